```python
import math
import jax, jax.numpy as jnp
from jax import lax
import numpy as np

D_MODEL = 1024
BATCH = 32
SEQ = 2048
DEPTH = 1
DEC_BATCH = 8
DEC_SEQ = 16
PAST_LEN = 2048

CHUNK = 64
D_CONV = 1024
CONV_W = 3
N_RET_HEADS = 4
RET_DK = 256
RET_DV = 512
D_RET_QK = N_RET_HEADS * RET_DK
D_RET_V = N_RET_HEADS * RET_DV
D_FF = 2816
ROPE_BASE = 10000.0
LN_EPS = 1e-5
GN_EPS = 1e-6
ALPHA = (2 * DEPTH) ** 0.25
BETA = (8 * DEPTH) ** -0.25
SPLITS = (D_CONV, D_CONV, D_CONV, D_RET_QK, D_RET_QK, D_RET_V, D_RET_V, D_MODEL, D_MODEL)
D_IN = sum(SPLITS)

kernel_name = "hybrid_shortconv_retention_macaron_deepnorm_step"


def layer_norm(x, g, b):
    xf = x.astype(jnp.float32)
    mu = xf.mean(-1, keepdims=True)
    var = jnp.square(xf - mu).mean(-1, keepdims=True)
    return ((xf - mu) * lax.rsqrt(var + LN_EPS)).astype(x.dtype) * g + b


def swiglu(x, w_up, w_down):
    a, b = jnp.split(x @ w_up, 2, axis=-1)
    return (jax.nn.silu(a) * b) @ w_down


def rotary(x, pos):
    half = x.shape[-1] // 2
    inv_freq = ROPE_BASE ** (-jnp.arange(half, dtype=jnp.float32) / half)
    ang = pos[:, None] * inv_freq[None, :]
    cos = jnp.cos(ang)[None, :, None, :].astype(x.dtype)
    sin = jnp.sin(ang)[None, :, None, :].astype(x.dtype)
    x1, x2 = x[..., :half], x[..., half:]
    return jnp.concatenate([x1 * cos - x2 * sin, x1 * sin + x2 * cos], axis=-1)


def retention_log_gamma():
    return jnp.log(1.0 - 2.0 ** (-5.0 - jnp.arange(N_RET_HEADS, dtype=jnp.float32)))


def retention_block(q, k, v, S, lg):
    L = q.shape[1]
    idx = jnp.arange(L, dtype=jnp.float32)
    diff = idx[:, None] - idx[None, :]
    decay = jnp.where(diff[None] >= 0,
                      jnp.exp(lg[:, None, None] * jnp.maximum(diff, 0.0)[None]), 0.0)
    scores = jnp.einsum('blhd,bmhd->bhlm', q, k) * decay[None]
    intra = jnp.einsum('bhlm,bmhe->blhe', scores, v)
    cross_dec = jnp.exp(lg[None, :] * (idx[:, None] + 1.0))
    cross = jnp.einsum('blhd,bhde->blhe', q, S) * cross_dec[None, :, :, None]
    k_dec = k * jnp.exp(lg[None, :] * (L - 1.0 - idx[:, None]))[None, :, :, None]
    S_new = jnp.exp(lg * L)[None, :, None, None] * S + jnp.einsum('blhd,blhe->bhde', k_dec, v)
    return intra + cross, S_new


def retention_seq(q, k, v, S0):
    lg = retention_log_gamma()
    B, T, H, _ = q.shape
    if T <= CHUNK:
        return retention_block(q, k, v, S0, lg)
    n = T // CHUNK

    def to_chunks(a):
        return a.reshape(B, n, CHUNK, H, a.shape[-1]).swapaxes(0, 1)

    def step(S, qkv):
        qc, kc, vc = qkv
        o, S = retention_block(qc, kc, vc, S, lg)
        return S, o

    S_fin, o = lax.scan(step, S0, (to_chunks(q), to_chunks(k), to_chunks(v)))
    o = o.swapaxes(0, 1).reshape(B, T, H, o.shape[-1])
    return o, S_fin


def short_conv(u, buf, w):
    T = u.shape[1]
    ext = jnp.concatenate([buf.astype(u.dtype), u], axis=1)
    y = sum(ext[:, j:j + T] * w[j] for j in range(CONV_W))
    return y, ext[:, -(CONV_W - 1):]


def mixer(x, pos, conv_buf, S0, w_in, conv_w, w_conv_br, w_ret_br, w_out):
    B, T, _ = x.shape
    offsets = np.cumsum(SPLITS)[:-1].tolist()
    cb, cc, ch, q, k, v, g, gc, gr = jnp.split(x @ w_in, offsets, axis=-1)
    yc, new_buf = short_conv(cc * ch, conv_buf, conv_w)
    y_conv = (cb * yc) @ w_conv_br
    qh = rotary(q.reshape(B, T, N_RET_HEADS, RET_DK), pos).astype(jnp.float32)
    kh = (rotary(k.reshape(B, T, N_RET_HEADS, RET_DK), pos) * (RET_DK ** -0.5)).astype(jnp.float32)
    vh = v.reshape(B, T, N_RET_HEADS, RET_DV).astype(jnp.float32)
    o, S_new = retention_seq(qh, kh, vh, S0.astype(jnp.float32))
    mu = o.mean(-1, keepdims=True)
    var = jnp.square(o - mu).mean(-1, keepdims=True)
    o = ((o - mu) * lax.rsqrt(var + GN_EPS)).astype(x.dtype).reshape(B, T, D_RET_V)
    y_ret = (o * jax.nn.silu(g)) @ w_ret_br
    m = jax.nn.sigmoid(gc) * y_conv + jax.nn.sigmoid(gr) * y_ret
    return m @ w_out, new_buf, S_new.astype(x.dtype)


def layer(x, pos, conv_buf, S0, ln1_g, ln1_b, ffn1_w_up, ffn1_w_down, w_in, conv_w, w_conv_br,
          w_ret_br, w_out, ln2_g, ln2_b, ffn2_w_up, ffn2_w_down, ln3_g, ln3_b):
    x = layer_norm(ALPHA * x + 0.5 * swiglu(x, ffn1_w_up, ffn1_w_down), ln1_g, ln1_b)
    h, new_buf, S_new = mixer(x, pos, conv_buf, S0, w_in, conv_w, w_conv_br, w_ret_br, w_out)
    x = layer_norm(ALPHA * x + h, ln2_g, ln2_b)
    x = layer_norm(ALPHA * x + 0.5 * swiglu(x, ffn2_w_up, ffn2_w_down), ln3_g, ln3_b)
    return x, new_buf, S_new


def setup_inputs(seed: int = 0) -> dict:
    key = jax.random.key(seed)
    ks = jax.random.split(key, 24)
    f32 = jnp.float32

    def nrm(k, shape, scale):
        return jax.random.normal(k, shape, f32) * scale

    def gain(k):
        return 1.0 + nrm(k, (DEPTH, D_MODEL), 0.02)

    def bias(k):
        return nrm(k, (DEPTH, D_MODEL), 0.02)

    col_scale = jnp.concatenate([
        jnp.full((n,), BETA if i in (2, 5) else 1.0, f32) for i, n in enumerate(SPLITS)])
    return {
        "x_prompt": nrm(ks[0], (BATCH, SEQ, D_MODEL), 1.0),
        "x_sample": nrm(ks[1], (DEC_BATCH, DEC_SEQ, D_MODEL), 1.0),
        "cache_conv": nrm(ks[2], (DEPTH, DEC_BATCH, CONV_W - 1, D_CONV), 1.0),
        "state_ret": nrm(ks[3], (DEPTH, DEC_BATCH, N_RET_HEADS, RET_DK, RET_DV), 1.0),
        "ln1_g": gain(ks[4]),
        "ln1_b": bias(ks[5]),
        "ffn1_w_up": nrm(ks[6], (DEPTH, D_MODEL, 2 * D_FF), D_MODEL ** -0.5),
        "ffn1_w_down": nrm(ks[7], (DEPTH, D_FF, D_MODEL), BETA * D_FF ** -0.5),
        "w_in": nrm(ks[8], (DEPTH, D_MODEL, D_IN), D_MODEL ** -0.5) * col_scale,
        "conv_w": nrm(ks[9], (DEPTH, CONV_W, D_CONV), CONV_W ** -0.5),
        "w_conv_br": nrm(ks[10], (DEPTH, D_CONV, D_MODEL), BETA * D_CONV ** -0.5),
        "w_ret_br": nrm(ks[11], (DEPTH, D_RET_V, D_MODEL), BETA * D_RET_V ** -0.5),
        "w_out": nrm(ks[12], (DEPTH, D_MODEL, D_MODEL), BETA * D_MODEL ** -0.5),
        "ln2_g": gain(ks[13]),
        "ln2_b": bias(ks[14]),
        "ffn2_w_up": nrm(ks[15], (DEPTH, D_MODEL, 2 * D_FF), D_MODEL ** -0.5),
        "ffn2_w_down": nrm(ks[16], (DEPTH, D_FF, D_MODEL), BETA * D_FF ** -0.5),
        "ln3_g": gain(ks[17]),
        "ln3_b": bias(ks[18]),
    }


def reference(x_prompt, x_sample, cache_conv, state_ret, ln1_g, ln1_b, ffn1_w_up, ffn1_w_down,
              w_in, conv_w, w_conv_br, w_ret_br, w_out, ln2_g, ln2_b, ffn2_w_up, ffn2_w_down,
              ln3_g, ln3_b):
    Bp, Tp, _ = x_prompt.shape
    Ts = x_sample.shape[1]
    pos_p = jnp.arange(Tp, dtype=jnp.float32)
    pos_s = PAST_LEN + jnp.arange(Ts, dtype=jnp.float32)
    yp, ys = x_prompt, x_sample
    conv_p, ret_p, conv_s, ret_s = [], [], [], []
    for l in range(DEPTH):
        params = (ln1_g[l], ln1_b[l], ffn1_w_up[l], ffn1_w_down[l], w_in[l], conv_w[l],
                  w_conv_br[l], w_ret_br[l], w_out[l], ln2_g[l], ln2_b[l], ffn2_w_up[l],
                  ffn2_w_down[l], ln3_g[l], ln3_b[l])
        buf0 = jnp.zeros((Bp, CONV_W - 1, D_CONV), x_prompt.dtype)
        S0 = jnp.zeros((Bp, N_RET_HEADS, RET_DK, RET_DV), jnp.float32)
        yp, cbp, sp = layer(yp, pos_p, buf0, S0, *params)
        ys, cbs, ss = layer(ys, pos_s, cache_conv[l], state_ret[l], *params)
        conv_p.append(cbp)
        ret_p.append(sp)
        conv_s.append(cbs)
        ret_s.append(ss)
    conv_prompt = jnp.stack(conv_p)
    ret_prompt = jnp.stack(ret_p)
    conv_sample = jnp.stack(conv_s)
    ret_sample = jnp.stack(ret_s)
    return (yp, ys, conv_prompt, ret_prompt, conv_sample, ret_sample)
```

```python
import functools
import math

import jax
import jax.numpy as jnp
from jax import lax
from jax.experimental import pallas as pl
from jax.experimental.pallas import tpu as pltpu

D_MODEL = 1024
D_FF = 2816
D_CONV = 1024
CONV_W = 3
N_HEADS = 4
DK = 256
DV = 512
D_QK = N_HEADS * DK
D_V = N_HEADS * DV
PAST_LEN = 2048
ROPE_BASE = 10000.0
LN_EPS = 1e-5
GN_EPS = 1e-6
DEPTH = 1
ALPHA = (2 * DEPTH) ** 0.25

OFF_CB = 0
OFF_CC = OFF_CB + D_CONV
OFF_CH = OFF_CC + D_CONV
OFF_Q = OFF_CH + D_CONV
OFF_K = OFF_Q + D_QK
OFF_V = OFF_K + D_QK
OFF_G = OFF_V + D_V
OFF_GC = OFF_G + D_V
OFF_GR = OFF_GC + D_MODEL
D_IN = OFF_GR + D_MODEL

LOG_GAMMA = tuple(math.log(1.0 - 2.0 ** (-5.0 - h)) for h in range(N_HEADS))

FFN_ROWS = 512
MIXER_ROWS = 256
VMEM_LIMIT_BYTES = 58 * 1024 * 1024

F32 = jnp.float32
BF16 = jnp.bfloat16


def _layer_norm(z, g, b):
    mu = jnp.mean(z, axis=-1, keepdims=True)
    d = z - mu
    var = jnp.mean(d * d, axis=-1, keepdims=True)
    return d * lax.rsqrt(var + LN_EPS) * g + b


def _dot(a, b):
    return jnp.dot(a, b, preferred_element_type=F32)


def _ffn_ln_kernel(x_ref, wu_ref, wd_ref, g_ref, b_ref, o_ref):
    x = x_ref[...]
    xb = x.astype(BF16)
    a = _dot(xb, wu_ref[:, :D_FF])
    b = _dot(xb, wu_ref[:, D_FF:])
    act = (a * jax.nn.sigmoid(a) * b).astype(BF16)
    y = _dot(act, wd_ref[...])
    o_ref[...] = _layer_norm(ALPHA * x + 0.5 * y, g_ref[...], b_ref[...])


def _const_spec(shape):
    zeros = (0,) * len(shape)
    return pl.BlockSpec(shape, lambda *_: zeros, pipeline_mode=pl.Buffered(1))


def _ffn_ln(x2d, w_up, w_down, g, b, rows):
    n = x2d.shape[0]
    assert n % rows == 0
    return pl.pallas_call(
        _ffn_ln_kernel,
        grid=(n // rows,),
        in_specs=[
            pl.BlockSpec((rows, D_MODEL), lambda i: (i, 0)),
            _const_spec((D_MODEL, 2 * D_FF)),
            _const_spec((D_FF, D_MODEL)),
            _const_spec((1, D_MODEL)),
            _const_spec((1, D_MODEL)),
        ],
        out_specs=pl.BlockSpec((rows, D_MODEL), lambda i: (i, 0)),
        out_shape=jax.ShapeDtypeStruct((n, D_MODEL), F32),
        compiler_params=pltpu.CompilerParams(
            dimension_semantics=("arbitrary",), vmem_limit_bytes=VMEM_LIMIT_BYTES),
        name="ffn_ln",
    )(x2d, w_up, w_down, g, b)


def _mixer_kernel(*refs, T, pos0, zero_init):
    if zero_init:
        (x_ref, win_ref, convw_ref, wcb_ref, wrb_ref, wout_ref, g_ref, b_ref,
         y_ref, cbuf_ref, s_ref) = refs
    else:
        (x_ref, win_ref, convw_ref, wcb_ref, wrb_ref, wout_ref, g_ref, b_ref, cbuf0_ref, s0_ref,
         y_ref, cbuf_ref, s_ref) = refs
    t = pl.program_id(1)

    @pl.when(t == 0)
    def _():
        if zero_init:
            cbuf_ref[...] = jnp.zeros_like(cbuf_ref)
            s_ref[...] = jnp.zeros_like(s_ref)
        else:
            cbuf_ref[...] = cbuf0_ref[...]
            s_ref[...] = s0_ref[...]

    x = x_ref[0]
    xb = x.astype(BF16)

    def proj(lo, width):
        return _dot(xb, win_ref[:, lo:lo + width])

    u = proj(OFF_CC, D_CONV) * proj(OFF_CH, D_CONV)
    prev = cbuf_ref[0]
    row = lax.broadcasted_iota(jnp.int32, (T, D_CONV), 0)
    u1 = jnp.where(row == 0, prev[1:2], pltpu.roll(u, 1, 0))
    u2 = jnp.where(row == 0, prev[0:1], jnp.where(row == 1, prev[1:2], pltpu.roll(u, 2, 0)))
    cw = convw_ref[...]
    yconv = u2 * cw[0:1] + u1 * cw[1:2] + u * cw[2:3]
    cbuf_ref[0] = u[T - 2:T]
    z = (proj(OFF_CB, D_CONV) * yconv).astype(BF16)
    m_conv = jax.nn.sigmoid(proj(OFF_GC, D_MODEL)) * _dot(z, wcb_ref[...])

    half = DK // 2
    frame = lax.broadcasted_iota(jnp.int32, (T, half), 0)
    pos = (frame + (pos0 + t * T)).astype(F32)
    lane = lax.broadcasted_iota(jnp.int32, (T, half), 1).astype(F32)
    ang = pos * jnp.exp(lane * (-math.log(ROPE_BASE) / half))
    cos = jnp.cos(ang)
    sin = jnp.sin(ang)

    def rotary(a):
        a1, a2 = a[:, :half], a[:, half:]
        return jnp.concatenate([a1 * cos - a2 * sin, a1 * sin + a2 * cos], axis=-1)

    q = proj(OFF_Q, D_QK)
    k = proj(OFF_K, D_QK)
    v = proj(OFF_V, D_V)
    g = proj(OFF_G, D_V)

    li = lax.broadcasted_iota(jnp.int32, (T, T), 0)
    mi = lax.broadcasted_iota(jnp.int32, (T, T), 1)
    diff = (li - mi).astype(F32)
    idx = lax.broadcasted_iota(jnp.int32, (T, 1), 0).astype(F32)

    gated = []
    for h in range(N_HEADS):
        lg = LOG_GAMMA[h]
        qh = rotary(q[:, h * DK:(h + 1) * DK])
        kh = rotary(k[:, h * DK:(h + 1) * DK]) * (DK ** -0.5)
        vb = v[:, h * DV:(h + 1) * DV].astype(BF16)
        qb = qh.astype(BF16)
        decay = jnp.where(diff >= 0, jnp.exp(lg * jnp.maximum(diff, 0.0)), 0.0)
        scores = lax.dot_general(qb, kh.astype(BF16), (((1,), (1,)), ((), ())),
                                 preferred_element_type=F32) * decay
        intra = _dot(scores.astype(BF16), vb)
        s_old = s_ref[0, h]
        cross = _dot(qb, s_old.astype(BF16)) * jnp.exp(lg * (idx + 1.0))
        o = intra + cross
        kdec = (kh * jnp.exp(lg * (T - 1.0 - idx))).astype(BF16)
        s_ref[0, h] = math.exp(lg * T) * s_old + lax.dot_general(
            kdec, vb, (((0,), (0,)), ((), ())), preferred_element_type=F32)
        mu = jnp.mean(o, axis=-1, keepdims=True)
        d = o - mu
        var = jnp.mean(d * d, axis=-1, keepdims=True)
        gh = g[:, h * DV:(h + 1) * DV]
        gated.append((d * lax.rsqrt(var + GN_EPS) * (gh * jax.nn.sigmoid(gh))).astype(BF16))
    y_ret = _dot(jnp.concatenate(gated, axis=-1), wrb_ref[...])

    m = m_conv + jax.nn.sigmoid(proj(OFF_GR, D_MODEL)) * y_ret
    hmix = _dot(m.astype(BF16), wout_ref[...])
    y_ref[0] = _layer_norm(ALPHA * x + hmix, g_ref[...], b_ref[...])


def _mixer(x, w_in, conv_w, w_conv_br, w_ret_br, w_out, g, b, T, pos0, init=None):
    B, seq, _ = x.shape
    assert seq % T == 0
    zero_init = init is None
    in_specs = [
        pl.BlockSpec((1, T, D_MODEL), lambda i, t: (i, t, 0)),
        _const_spec((D_MODEL, D_IN)),
        _const_spec((CONV_W, D_CONV)),
        _const_spec((D_CONV, D_MODEL)),
        _const_spec((D_V, D_MODEL)),
        _const_spec((D_MODEL, D_MODEL)),
        _const_spec((1, D_MODEL)),
        _const_spec((1, D_MODEL)),
    ]
    args = [x, w_in, conv_w, w_conv_br, w_ret_br, w_out, g, b]
    cbuf_spec = pl.BlockSpec((1, CONV_W - 1, D_CONV), lambda i, t: (i, 0, 0))
    s_spec = pl.BlockSpec((1, N_HEADS, DK, DV), lambda i, t: (i, 0, 0, 0))
    if not zero_init:
        in_specs += [cbuf_spec, s_spec]
        args += list(init)
    return pl.pallas_call(
        functools.partial(_mixer_kernel, T=T, pos0=pos0, zero_init=zero_init),
        grid=(B, seq // T),
        in_specs=in_specs,
        out_specs=[pl.BlockSpec((1, T, D_MODEL), lambda i, t: (i, t, 0)), cbuf_spec, s_spec],
        out_shape=[
            jax.ShapeDtypeStruct((B, seq, D_MODEL), F32),
            jax.ShapeDtypeStruct((B, CONV_W - 1, D_CONV), F32),
            jax.ShapeDtypeStruct((B, N_HEADS, DK, DV), F32),
        ],
        compiler_params=pltpu.CompilerParams(
            dimension_semantics=("arbitrary", "arbitrary"), vmem_limit_bytes=VMEM_LIMIT_BYTES),
        name="mixer_prompt" if zero_init else "mixer_sample",
    )(*args)


def kernel(x_prompt, x_sample, cache_conv, state_ret, ln1_g, ln1_b, ffn1_w_up, ffn1_w_down, w_in,
           conv_w, w_conv_br, w_ret_br, w_out, ln2_g, ln2_b, ffn2_w_up, ffn2_w_down, ln3_g, ln3_b):
    assert w_in.shape[0] == DEPTH == 1
    Bp, Tp, _ = x_prompt.shape
    Bs, Ts, _ = x_sample.shape
    wu1, wd1 = ffn1_w_up[0].astype(BF16), ffn1_w_down[0].astype(BF16)
    wu2, wd2 = ffn2_w_up[0].astype(BF16), ffn2_w_down[0].astype(BF16)
    win = w_in[0].astype(BF16)
    wcb, wrb, wo = w_conv_br[0].astype(BF16), w_ret_br[0].astype(BF16), w_out[0].astype(BF16)

    def layer(x, T, pos0, rows, init):
        B, seq, _ = x.shape
        h = _ffn_ln(x.reshape(B * seq, D_MODEL), wu1, wd1, ln1_g, ln1_b, rows)
        h, cbuf, s = _mixer(h.reshape(B, seq, D_MODEL), win, conv_w[0], wcb, wrb, wo, ln2_g, ln2_b,
                            T, pos0, init)
        h = _ffn_ln(h.reshape(B * seq, D_MODEL), wu2, wd2, ln3_g, ln3_b, rows)
        return h.reshape(B, seq, D_MODEL), cbuf[None], s[None]

    yp, conv_p, ret_p = layer(x_prompt, MIXER_ROWS, 0, FFN_ROWS, None)
    ys, conv_s, ret_s = layer(x_sample, Ts, PAST_LEN, Bs * Ts, (cache_conv[0], state_ret[0]))
    return (yp, ys, conv_p, ret_p, conv_s, ret_s)
```

```python
import functools
import math

import jax
import jax.numpy as jnp
from jax import lax
from jax.experimental import pallas as pl
from jax.experimental.pallas import tpu as pltpu

D_MODEL = 1024
D_FF = 2816
D_CONV = 1024
CONV_W = 3
N_HEADS = 4
DK = 256
DV = 512
D_QK = N_HEADS * DK
D_V = N_HEADS * DV
PAST_LEN = 2048
ROPE_BASE = 10000.0
LN_EPS = 1e-5
GN_EPS = 1e-6
DEPTH = 1
ALPHA = (2 * DEPTH) ** 0.25

OFF_CB = 0
OFF_CC = OFF_CB + D_CONV
OFF_CH = OFF_CC + D_CONV
OFF_Q = OFF_CH + D_CONV
OFF_K = OFF_Q + D_QK
OFF_V = OFF_K + D_QK
OFF_G = OFF_V + D_V
OFF_GC = OFF_G + D_V
OFF_GR = OFF_GC + D_MODEL
D_IN = OFF_GR + D_MODEL

LOG_GAMMA = tuple(math.log(1.0 - 2.0 ** (-5.0 - h)) for h in range(N_HEADS))

LANES = 128
LANE_PAD = LANES
FFN_ROWS = 1024
FFN_SUBTILES = 2
MIXER_ROWS = 256
VMEM_LIMIT_BYTES = 58 * 1024 * 1024

F32 = jnp.float32
BF16 = jnp.bfloat16


def _layer_norm(z, g, b):
    mu = jnp.mean(z, axis=-1, keepdims=True)
    d = z - mu
    var = jnp.mean(d * d, axis=-1, keepdims=True)
    return d * lax.rsqrt(var + LN_EPS) * g + b


def _dot(a, b):
    return jnp.dot(a, b, preferred_element_type=F32)


def _ffn_ln_kernel(x_ref, wu_ref, wd_ref, g_ref, b_ref, o_ref, z_scr, *, n_tiles):
    i = pl.program_id(0)

    def norm_previous():
        o_ref[...] = _layer_norm(z_scr[...], g_ref[...], b_ref[...])

    @pl.when(i == 0)
    def _():
        z_scr[...] = jnp.zeros_like(z_scr)

    @pl.when(i < n_tiles)
    def _():
        norm_previous()
        sub = x_ref.shape[0] // FFN_SUBTILES
        for r in range(FFN_SUBTILES):
            x = x_ref[r * sub:(r + 1) * sub, :]
            xb = x.astype(BF16)
            a = _dot(xb, wu_ref[:, :D_FF])
            b = _dot(xb, wu_ref[:, D_FF:])
            act = (a * jax.nn.sigmoid(a) * b).astype(BF16)
            y = _dot(act, wd_ref[:, :D_MODEL])
            z_scr[r * sub:(r + 1) * sub, :] = ALPHA * x + 0.5 * y

    @pl.when(i == n_tiles)
    def _():
        norm_previous()


def _const_spec(shape):
    zeros = (0,) * len(shape)
    return pl.BlockSpec(shape, lambda *_: zeros, pipeline_mode=pl.Buffered(1))


def _ffn_ln(x2d, w_up, w_down, g, b, rows):
    n = x2d.shape[0]
    assert n % rows == 0
    n_tiles = n // rows
    return pl.pallas_call(
        functools.partial(_ffn_ln_kernel, n_tiles=n_tiles),
        grid=(n_tiles + 1,),
        in_specs=[
            pl.BlockSpec((rows, D_MODEL), lambda i: (jnp.minimum(i, n_tiles - 1), 0)),
            _const_spec((D_MODEL, 2 * D_FF)),
            _const_spec((D_FF, D_MODEL + LANE_PAD)),
            _const_spec((1, D_MODEL)),
            _const_spec((1, D_MODEL)),
        ],
        out_specs=pl.BlockSpec((rows, D_MODEL), lambda i: (jnp.maximum(i - 1, 0), 0)),
        out_shape=jax.ShapeDtypeStruct((n, D_MODEL), F32),
        scratch_shapes=[pltpu.VMEM((rows, D_MODEL), F32)],
        compiler_params=pltpu.CompilerParams(
            dimension_semantics=("arbitrary",), vmem_limit_bytes=VMEM_LIMIT_BYTES),
        name="ffn_ln",
    )(x2d, w_up, w_down, g, b)


def _mixer_kernel(*refs, T, pos0, zero_init):
    if zero_init:
        (x_ref, win_ref, convw_ref, wcb_ref, wrb_ref, wout_ref, g_ref, b_ref,
         y_ref, cbuf_ref, s_ref, rope_scr, dec_scr, rowdec_scr) = refs
    else:
        (x_ref, win_ref, convw_ref, wcb_ref, wrb_ref, wout_ref, g_ref, b_ref, cbuf0_ref, s0_ref,
         y_ref, cbuf_ref, s_ref, rope_scr, dec_scr, rowdec_scr) = refs
    t = pl.program_id(1)

    @pl.when(t == 0)
    def _():
        if zero_init:
            cbuf_ref[...] = jnp.zeros_like(cbuf_ref)
            s_ref[...] = jnp.zeros_like(s_ref)
        else:
            cbuf_ref[...] = cbuf0_ref[...]
            s_ref[...] = s0_ref[...]

    half = DK // 2
    b = pl.program_id(0)

    @pl.when(b == 0)
    def _():
        frame = lax.broadcasted_iota(jnp.int32, (T, half), 0)
        pos = (frame + (pos0 + t * T)).astype(F32)
        lane = lax.broadcasted_iota(jnp.int32, (T, half), 1).astype(F32)
        ang = pos * jnp.exp(lane * (-math.log(ROPE_BASE) / half))
        rows = pl.ds(pl.multiple_of(t * T, T), T)
        rope_scr[0, rows, :] = jnp.cos(ang)
        rope_scr[1, rows, :] = jnp.sin(ang)

    @pl.when((b == 0) & (t == 0))
    def _():
        li = lax.broadcasted_iota(jnp.int32, (T, T), 0)
        mi = lax.broadcasted_iota(jnp.int32, (T, T), 1)
        diff = (li - mi).astype(F32)
        idx = lax.broadcasted_iota(jnp.int32, (T, LANES), 0).astype(F32)
        for h in range(N_HEADS):
            lg = LOG_GAMMA[h]
            dec_scr[h] = jnp.where(diff >= 0, jnp.exp(lg * jnp.maximum(diff, 0.0)), 0.0)
            rowdec_scr[0, h] = jnp.exp(lg * (idx + 1.0))
            rowdec_scr[1, h] = jnp.exp(lg * (T - 1.0 - idx))

    x = x_ref[0]
    xb = x.astype(BF16)

    def proj(lo, width):
        return _dot(xb, win_ref[:, lo:lo + width])

    u = proj(OFF_CC, D_CONV) * proj(OFF_CH, D_CONV)
    prev = cbuf_ref[0]
    row = lax.broadcasted_iota(jnp.int32, (T, D_CONV), 0)
    u1 = jnp.where(row == 0, prev[1:2], pltpu.roll(u, 1, 0))
    u2 = jnp.where(row == 0, prev[0:1], jnp.where(row == 1, prev[1:2], pltpu.roll(u, 2, 0)))
    cw = convw_ref[...]
    yconv = u2 * cw[0:1] + u1 * cw[1:2] + u * cw[2:3]
    cbuf_ref[0] = u[T - 2:T]
    z = (proj(OFF_CB, D_CONV) * yconv).astype(BF16)
    m_conv = jax.nn.sigmoid(proj(OFF_GC, D_MODEL)) * _dot(z, wcb_ref[:, :D_MODEL])

    rows = pl.ds(pl.multiple_of(t * T, T), T)
    cos = rope_scr[0, rows, :]
    sin = rope_scr[1, rows, :]

    def rotary(a):
        a1, a2 = a[:, :half], a[:, half:]
        return jnp.concatenate([a1 * cos - a2 * sin, a1 * sin + a2 * cos], axis=-1)

    def lane_tiled(a, width):
        return jnp.concatenate([a] * (width // LANES), axis=-1)

    q = proj(OFF_Q, D_QK)
    k = proj(OFF_K, D_QK)
    v = proj(OFF_V, D_V)
    g = proj(OFF_G, D_V)

    gated = []
    for h in range(N_HEADS):
        lg = LOG_GAMMA[h]
        qh = rotary(q[:, h * DK:(h + 1) * DK])
        kh = rotary(k[:, h * DK:(h + 1) * DK]) * (DK ** -0.5)
        vb = v[:, h * DV:(h + 1) * DV].astype(BF16)
        qb = qh.astype(BF16)
        scores = lax.dot_general(qb, kh.astype(BF16), (((1,), (1,)), ((), ())),
                                 preferred_element_type=F32) * dec_scr[h]
        intra = _dot(scores.astype(BF16), vb)
        s_old = s_ref[0, h]
        cross = _dot(qb, s_old.astype(BF16)) * lane_tiled(rowdec_scr[0, h], DV)
        o = intra + cross
        kdec = (kh * lane_tiled(rowdec_scr[1, h], DK)).astype(BF16)
        s_ref[0, h] = math.exp(lg * T) * s_old + lax.dot_general(
            kdec, vb, (((0,), (0,)), ((), ())), preferred_element_type=F32)
        mu = jnp.mean(o, axis=-1, keepdims=True)
        d = o - mu
        var = jnp.mean(d * d, axis=-1, keepdims=True)
        gh = g[:, h * DV:(h + 1) * DV]
        gated.append((d * lax.rsqrt(var + GN_EPS) * (gh * jax.nn.sigmoid(gh))).astype(BF16))
    y_ret = _dot(jnp.concatenate(gated, axis=-1), wrb_ref[:, :D_MODEL])

    m = m_conv + jax.nn.sigmoid(proj(OFF_GR, D_MODEL)) * y_ret
    hmix = _dot(m.astype(BF16), wout_ref[:, :D_MODEL])
    y_ref[0] = _layer_norm(ALPHA * x + hmix, g_ref[...], b_ref[...])


def _mixer(x, w_in, conv_w, w_conv_br, w_ret_br, w_out, g, b, T, pos0, init=None):
    B, seq, _ = x.shape
    assert seq % T == 0
    zero_init = init is None
    in_specs = [
        pl.BlockSpec((1, T, D_MODEL), lambda i, t: (i, t, 0)),
        _const_spec((D_MODEL, D_IN + LANE_PAD)),
        _const_spec((CONV_W, D_CONV)),
        _const_spec((D_CONV, D_MODEL + LANE_PAD)),
        _const_spec((D_V, D_MODEL + LANE_PAD)),
        _const_spec((D_MODEL, D_MODEL + LANE_PAD)),
        _const_spec((1, D_MODEL)),
        _const_spec((1, D_MODEL)),
    ]
    args = [x, w_in, conv_w, w_conv_br, w_ret_br, w_out, g, b]
    cbuf_spec = pl.BlockSpec((1, CONV_W - 1, D_CONV), lambda i, t: (i, 0, 0))
    s_spec = pl.BlockSpec((1, N_HEADS, DK, DV), lambda i, t: (i, 0, 0, 0))
    if not zero_init:
        in_specs += [cbuf_spec, s_spec]
        args += list(init)
    return pl.pallas_call(
        functools.partial(_mixer_kernel, T=T, pos0=pos0, zero_init=zero_init),
        grid=(B, seq // T),
        in_specs=in_specs,
        out_specs=[pl.BlockSpec((1, T, D_MODEL), lambda i, t: (i, t, 0)), cbuf_spec, s_spec],
        out_shape=[
            jax.ShapeDtypeStruct((B, seq, D_MODEL), F32),
            jax.ShapeDtypeStruct((B, CONV_W - 1, D_CONV), F32),
            jax.ShapeDtypeStruct((B, N_HEADS, DK, DV), F32),
        ],
        scratch_shapes=[
            pltpu.VMEM((2, seq, DK // 2), F32),
            pltpu.VMEM((N_HEADS, T, T), F32),
            pltpu.VMEM((2, N_HEADS, T, LANES), F32),
        ],
        compiler_params=pltpu.CompilerParams(
            dimension_semantics=("arbitrary", "arbitrary"), vmem_limit_bytes=VMEM_LIMIT_BYTES),
        name="mixer_prompt" if zero_init else "mixer_sample",
    )(*args)


def kernel(x_prompt, x_sample, cache_conv, state_ret, ln1_g, ln1_b, ffn1_w_up, ffn1_w_down, w_in,
           conv_w, w_conv_br, w_ret_br, w_out, ln2_g, ln2_b, ffn2_w_up, ffn2_w_down, ln3_g, ln3_b):
    assert w_in.shape[0] == DEPTH == 1
    Bp, Tp, _ = x_prompt.shape
    Bs, Ts, _ = x_sample.shape

    def padded(w):
        return jnp.pad(w[0].astype(BF16), ((0, 0), (0, LANE_PAD)))

    wu1, wd1 = ffn1_w_up[0].astype(BF16), padded(ffn1_w_down)
    wu2, wd2 = ffn2_w_up[0].astype(BF16), padded(ffn2_w_down)
    win, wcb, wrb, wo = padded(w_in), padded(w_conv_br), padded(w_ret_br), padded(w_out)

    def layer(x, T, pos0, rows, init):
        B, seq, _ = x.shape
        h = _ffn_ln(x.reshape(B * seq, D_MODEL), wu1, wd1, ln1_g, ln1_b, rows)
        h, cbuf, s = _mixer(h.reshape(B, seq, D_MODEL), win, conv_w[0], wcb, wrb, wo, ln2_g, ln2_b,
                            T, pos0, init)
        h = _ffn_ln(h.reshape(B * seq, D_MODEL), wu2, wd2, ln3_g, ln3_b, rows)
        return h.reshape(B, seq, D_MODEL), cbuf[None], s[None]

    yp, conv_p, ret_p = layer(x_prompt, MIXER_ROWS, 0, FFN_ROWS, None)
    ys, conv_s, ret_s = layer(x_sample, Ts, PAST_LEN, Bs * Ts, (cache_conv[0], state_ret[0]))
    return (yp, ys, conv_p, ret_p, conv_s, ret_s)
```

```python
import functools
import math

import jax
import jax.numpy as jnp
from jax import lax
from jax.experimental import pallas as pl
from jax.experimental.pallas import tpu as pltpu

D_MODEL = 1024
D_FF = 2816
D_CONV = 1024
CONV_W = 3
N_HEADS = 4
DK = 256
DV = 512
D_QK = N_HEADS * DK
D_V = N_HEADS * DV
PAST_LEN = 2048
ROPE_BASE = 10000.0
LN_EPS = 1e-5
GN_EPS = 1e-6
DEPTH = 1
ALPHA = (2 * DEPTH) ** 0.25
K_SCALE = DK ** -0.5
assert math.frexp(K_SCALE)[0] == 0.5, "the key scale is folded into the rotary tables"

OFF_CB = 0
OFF_CC = OFF_CB + D_CONV
OFF_CH = OFF_CC + D_CONV
OFF_Q = OFF_CH + D_CONV
OFF_K = OFF_Q + D_QK
OFF_V = OFF_K + D_QK
OFF_G = OFF_V + D_V
OFF_GC = OFF_G + D_V
OFF_GR = OFF_GC + D_MODEL
D_IN = OFF_GR + D_MODEL

LOG_GAMMA = tuple(math.log(1.0 - 2.0 ** (-5.0 - h)) for h in range(N_HEADS))

LANES = 128
LANE_PAD = LANES
FFN_ROWS = 1024
MIXER_ROWS = 256
VMEM_LIMIT_BYTES = 58 * 1024 * 1024

F32 = jnp.float32
BF16 = jnp.bfloat16


def _layer_norm(z, g, b):
    mu = jnp.mean(z, axis=-1, keepdims=True)
    d = z - mu
    var = jnp.mean(d * d, axis=-1, keepdims=True)
    return d * lax.rsqrt(var + LN_EPS) * g + b


def _dot(a, b):
    return jnp.dot(a, b, preferred_element_type=F32)


def _ffn_ln_kernel(x_ref, xnext_ref, wu_ref, wd_ref, g_ref, b_ref, o_ref, act_scr):
    half = x_ref.shape[0] // 2

    def gated_up(x):
        xb = x.astype(BF16)
        a = _dot(xb, wu_ref[:, :D_FF])
        b = _dot(xb, wu_ref[:, D_FF:])
        return (a * jax.nn.sigmoid(a) * b).astype(BF16)

    def down(act):
        return _dot(act, wd_ref[:, :D_MODEL])

    @pl.when(pl.program_id(0) == 0)
    def _():
        act_scr[...] = gated_up(x_ref[:half, :])

    x0 = x_ref[:half, :]
    x1 = x_ref[half:, :]
    y0 = down(act_scr[...])
    y1 = down(gated_up(x1))
    act_scr[...] = gated_up(xnext_ref[...])
    o_ref[:half, :] = _layer_norm(ALPHA * x0 + 0.5 * y0, g_ref[...], b_ref[...])
    o_ref[half:, :] = _layer_norm(ALPHA * x1 + 0.5 * y1, g_ref[...], b_ref[...])


def _const_spec(shape):
    zeros = (0,) * len(shape)
    return pl.BlockSpec(shape, lambda *_: zeros, pipeline_mode=pl.Buffered(1))


def _ffn_ln(x2d, w_up, w_down, g, b, rows):
    n = x2d.shape[0]
    assert n % rows == 0 and rows % 2 == 0
    n_tiles = n // rows
    half = rows // 2
    return pl.pallas_call(
        _ffn_ln_kernel,
        grid=(n_tiles,),
        in_specs=[
            pl.BlockSpec((rows, D_MODEL), lambda i: (i, 0)),
            pl.BlockSpec((half, D_MODEL), lambda i: (2 * jnp.minimum(i + 1, n_tiles - 1), 0)),
            _const_spec((D_MODEL, 2 * D_FF)),
            _const_spec((D_FF, D_MODEL + LANE_PAD)),
            _const_spec((1, D_MODEL)),
            _const_spec((1, D_MODEL)),
        ],
        out_specs=pl.BlockSpec((rows, D_MODEL), lambda i: (i, 0)),
        out_shape=jax.ShapeDtypeStruct((n, D_MODEL), F32),
        scratch_shapes=[pltpu.VMEM((half, D_FF), BF16)],
        compiler_params=pltpu.CompilerParams(
            dimension_semantics=("arbitrary",), vmem_limit_bytes=VMEM_LIMIT_BYTES),
        name="ffn_ln",
    )(x2d, x2d, w_up, w_down, g, b)


def _mixer_kernel(*refs, T, pos0, zero_init):
    if zero_init:
        (x_ref, xnext_ref, win_ref, convw_ref, wcb_ref, wrb_ref, wout_ref, g_ref, b_ref,
         y_ref, cbuf_ref, s_ref, rope_scr, dec_scr, rowdec_scr, qk_scr) = refs
    else:
        (x_ref, xnext_ref, win_ref, convw_ref, wcb_ref, wrb_ref, wout_ref, g_ref, b_ref,
         cbuf0_ref, s0_ref,
         y_ref, cbuf_ref, s_ref, rope_scr, dec_scr, rowdec_scr, qk_scr) = refs
    t = pl.program_id(1)

    @pl.when(t == 0)
    def _():
        if zero_init:
            cbuf_ref[...] = jnp.zeros_like(cbuf_ref)
            s_ref[...] = jnp.zeros_like(s_ref)
        else:
            cbuf_ref[...] = cbuf0_ref[...]
            s_ref[...] = s0_ref[...]

    half = DK // 2
    b = pl.program_id(0)

    @pl.when(b == 0)
    def _():
        frame = lax.broadcasted_iota(jnp.int32, (T, half), 0)
        pos = (frame + (pos0 + t * T)).astype(F32)
        lane = lax.broadcasted_iota(jnp.int32, (T, half), 1).astype(F32)
        ang = pos * jnp.exp(lane * (-math.log(ROPE_BASE) / half))
        rows = pl.ds(pl.multiple_of(t * T, T), T)
        rope_scr[0, rows, :] = jnp.cos(ang)
        rope_scr[1, rows, :] = jnp.sin(ang)
        rope_scr[2, rows, :] = jnp.cos(ang) * K_SCALE
        rope_scr[3, rows, :] = jnp.sin(ang) * K_SCALE

    @pl.when((b == 0) & (t == 0))
    def _():
        li = lax.broadcasted_iota(jnp.int32, (T, T), 0)
        mi = lax.broadcasted_iota(jnp.int32, (T, T), 1)
        diff = (li - mi).astype(F32)
        idx = lax.broadcasted_iota(jnp.int32, (T, LANES), 0).astype(F32)
        for h in range(N_HEADS):
            lg = LOG_GAMMA[h]
            dec_scr[h] = jnp.where(diff >= 0, jnp.exp(lg * jnp.maximum(diff, 0.0)), 0.0)
            rowdec_scr[0, h] = jnp.exp(lg * (idx + 1.0))
            rowdec_scr[1, h] = jnp.exp(lg * (T - 1.0 - idx))

    x = x_ref[0]
    xb = x.astype(BF16)

    rows = pl.ds(pl.multiple_of(t * T, T), T)

    def proj(lo, width, lhs=xb):
        return _dot(lhs, win_ref[:, lo:lo + width])

    @pl.when((b == 0) & (t == 0))
    def _():
        qk_scr[...] = proj(OFF_Q, 2 * D_QK)

    def rotary(a, plane):
        cos, sin = rope_scr[plane, rows, :], rope_scr[plane + 1, rows, :]
        a1, a2 = a[:, :half], a[:, half:]
        return jnp.concatenate([a1 * cos - a2 * sin, a1 * sin + a2 * cos], axis=-1)

    def lane_tiled(a, width):
        return jnp.concatenate([a] * (width // LANES), axis=-1)

    def conv_input():
        u = proj(OFF_CC, D_CONV) * proj(OFF_CH, D_CONV)
        prev = cbuf_ref[0]
        row = lax.broadcasted_iota(jnp.int32, (T, D_CONV), 0)
        u1 = jnp.where(row == 0, prev[1:2], pltpu.roll(u, 1, 0))
        u2 = jnp.where(row == 0, prev[0:1], jnp.where(row == 1, prev[1:2], pltpu.roll(u, 2, 0)))
        cw = convw_ref[...]
        cbuf_ref[0] = u[T - 2:T]
        return u2 * cw[0:1] + u1 * cw[1:2] + u * cw[2:3]

    def retention_scores(h, q, k):
        qh = rotary(q[:, h * DK:(h + 1) * DK], 0)
        kh = rotary(k[:, h * DK:(h + 1) * DK], 2)
        qb = qh.astype(BF16)
        scores = lax.dot_general(qb, kh.astype(BF16), (((1,), (1,)), ((), ())),
                                 preferred_element_type=F32) * dec_scr[h]
        return qh, kh, qb, scores.astype(BF16)

    def retention_output(h, qh, kh, qb, scores, v):
        lg = LOG_GAMMA[h]
        vb = v[:, h * DV:(h + 1) * DV].astype(BF16)
        s_old = s_ref[0, h]
        if T % LANES == 0:
            qc = (qh * lane_tiled(rowdec_scr[0, h], DK)).astype(BF16)
            o = _dot(jnp.concatenate([scores, qc], axis=-1),
                     jnp.concatenate([vb, s_old.astype(BF16)], axis=0))
        else:
            intra = _dot(scores, vb)
            cross = _dot(qb, s_old.astype(BF16)) * lane_tiled(rowdec_scr[0, h], DV)
            o = intra + cross
        kdec = (kh * lane_tiled(rowdec_scr[1, h], DK)).astype(BF16)
        s_ref[0, h] = math.exp(lg * T) * s_old + lax.dot_general(
            kdec, vb, (((0,), (0,)), ((), ())), preferred_element_type=F32)
        return o

    def normed_gated(o, gh):
        mu = jnp.mean(o, axis=-1, keepdims=True)
        d = o - mu
        var = jnp.mean(d * d, axis=-1, keepdims=True)
        return (d * lax.rsqrt(var + GN_EPS) * (gh * jax.nn.sigmoid(gh))).astype(BF16)

    q = qk_scr[:, :D_QK]
    k = qk_scr[:, D_QK:]
    v = proj(OFF_V, D_V)
    gated = []
    yconv = z = m_conv = gate_r = None
    for h in range(N_HEADS):
        head = retention_scores(h, q, k)
        gh = proj(OFF_G + h * DV, DV)
        o = retention_output(h, *head, v)
        if h == 0:
            yconv = conv_input()
        elif h == 1:
            z = (proj(OFF_CB, D_CONV) * yconv).astype(BF16)
        elif h == 2:
            m_conv = jax.nn.sigmoid(proj(OFF_GC, D_MODEL)) * _dot(z, wcb_ref[:, :D_MODEL])
        else:
            gate_r = jax.nn.sigmoid(proj(OFF_GR, D_MODEL))
        gated.append(normed_gated(o, gh))
    y_ret = _dot(jnp.concatenate(gated, axis=-1), wrb_ref[:, :D_MODEL])

    m = m_conv + gate_r * y_ret
    hmix = _dot(m.astype(BF16), wout_ref[:, :D_MODEL])
    qk_scr[...] = proj(OFF_Q, 2 * D_QK, xnext_ref[0].astype(BF16))
    y_ref[0] = _layer_norm(ALPHA * x + hmix, g_ref[...], b_ref[...])


def _mixer(x, w_in, conv_w, w_conv_br, w_ret_br, w_out, g, b, T, pos0, init=None):
    B, seq, _ = x.shape
    assert seq % T == 0
    zero_init = init is None
    n_t = seq // T

    def next_step(i, t):
        lin = jnp.minimum(i * n_t + t + 1, B * n_t - 1)
        return (lin // n_t, lin % n_t, 0)

    in_specs = [
        pl.BlockSpec((1, T, D_MODEL), lambda i, t: (i, t, 0)),
        pl.BlockSpec((1, T, D_MODEL), next_step),
        _const_spec((D_MODEL, D_IN + LANE_PAD)),
        _const_spec((CONV_W, D_CONV)),
        _const_spec((D_CONV, D_MODEL + LANE_PAD)),
        _const_spec((D_V, D_MODEL + LANE_PAD)),
        _const_spec((D_MODEL, D_MODEL + LANE_PAD)),
        _const_spec((1, D_MODEL)),
        _const_spec((1, D_MODEL)),
    ]
    args = [x, x, w_in, conv_w, w_conv_br, w_ret_br, w_out, g, b]
    cbuf_spec = pl.BlockSpec((1, CONV_W - 1, D_CONV), lambda i, t: (i, 0, 0))
    s_spec = pl.BlockSpec((1, N_HEADS, DK, DV), lambda i, t: (i, 0, 0, 0))
    if not zero_init:
        in_specs += [cbuf_spec, s_spec]
        args += list(init)
    return pl.pallas_call(
        functools.partial(_mixer_kernel, T=T, pos0=pos0, zero_init=zero_init),
        grid=(B, seq // T),
        in_specs=in_specs,
        out_specs=[pl.BlockSpec((1, T, D_MODEL), lambda i, t: (i, t, 0)), cbuf_spec, s_spec],
        out_shape=[
            jax.ShapeDtypeStruct((B, seq, D_MODEL), F32),
            jax.ShapeDtypeStruct((B, CONV_W - 1, D_CONV), F32),
            jax.ShapeDtypeStruct((B, N_HEADS, DK, DV), F32),
        ],
        scratch_shapes=[
            pltpu.VMEM((4, seq, DK // 2), F32),
            pltpu.VMEM((N_HEADS, T, T), F32),
            pltpu.VMEM((2, N_HEADS, T, LANES), F32),
            pltpu.VMEM((T, 2 * D_QK), F32),
        ],
        compiler_params=pltpu.CompilerParams(
            dimension_semantics=("arbitrary", "arbitrary"), vmem_limit_bytes=VMEM_LIMIT_BYTES),
        name="mixer_prompt" if zero_init else "mixer_sample",
    )(*args)


def kernel(x_prompt, x_sample, cache_conv, state_ret, ln1_g, ln1_b, ffn1_w_up, ffn1_w_down, w_in,
           conv_w, w_conv_br, w_ret_br, w_out, ln2_g, ln2_b, ffn2_w_up, ffn2_w_down, ln3_g, ln3_b):
    assert w_in.shape[0] == DEPTH == 1
    Bp, Tp, _ = x_prompt.shape
    Bs, Ts, _ = x_sample.shape

    def padded(w):
        return jnp.pad(w[0].astype(BF16), ((0, 0), (0, LANE_PAD)))

    wu1, wd1 = ffn1_w_up[0].astype(BF16), padded(ffn1_w_down)
    wu2, wd2 = ffn2_w_up[0].astype(BF16), padded(ffn2_w_down)
    win, wcb, wrb, wo = padded(w_in), padded(w_conv_br), padded(w_ret_br), padded(w_out)

    def layer(x, T, pos0, rows, init):
        B, seq, _ = x.shape
        h = _ffn_ln(x.reshape(B * seq, D_MODEL), wu1, wd1, ln1_g, ln1_b, rows)
        h, cbuf, s = _mixer(h.reshape(B, seq, D_MODEL), win, conv_w[0], wcb, wrb, wo, ln2_g, ln2_b,
                            T, pos0, init)
        h = _ffn_ln(h.reshape(B * seq, D_MODEL), wu2, wd2, ln3_g, ln3_b, rows)
        return h.reshape(B, seq, D_MODEL), cbuf[None], s[None]

    yp, conv_p, ret_p = layer(x_prompt, MIXER_ROWS, 0, FFN_ROWS, None)
    ys, conv_s, ret_s = layer(x_sample, Ts, PAST_LEN, Bs * Ts, (cache_conv[0], state_ret[0]))
    return (yp, ys, conv_p, ret_p, conv_s, ret_s)
```

```python
import functools
import math

import jax
import jax.numpy as jnp
from jax import lax
from jax.experimental import pallas as pl
from jax.experimental.pallas import tpu as pltpu

D_MODEL = 1024
D_FF = 2816
D_CONV = 1024
CONV_W = 3
N_HEADS = 4
DK = 256
DV = 512
D_QK = N_HEADS * DK
D_V = N_HEADS * DV
PAST_LEN = 2048
ROPE_BASE = 10000.0
LN_EPS = 1e-5
GN_EPS = 1e-6
DEPTH = 1
ALPHA = (2 * DEPTH) ** 0.25
K_SCALE = DK ** -0.5
assert math.frexp(K_SCALE)[0] == 0.5, "the key scale is folded into the rotary tables"

OFF_CB = 0
OFF_CC = OFF_CB + D_CONV
OFF_CH = OFF_CC + D_CONV
OFF_Q = OFF_CH + D_CONV
OFF_K = OFF_Q + D_QK
OFF_V = OFF_K + D_QK
OFF_G = OFF_V + D_V
OFF_GC = OFF_G + D_V
OFF_GR = OFF_GC + D_MODEL
D_IN = OFF_GR + D_MODEL

LOG_GAMMA = tuple(math.log(1.0 - 2.0 ** (-5.0 - h)) for h in range(N_HEADS))

LANES = 128
W_SPLIT = D_IN // 2
SLAB_COLS = W_SPLIT + 2 * D_MODEL
assert (SLAB_COLS // LANES) % 8 != 0 and D_V == 2 * D_MODEL
OFF_WCB = D_IN
OFF_WOUT = OFF_WCB + D_MODEL
OFF_WRB = OFF_WOUT + D_MODEL
FFN_ROWS = 1024
MIXER_ROWS = 256
VMEM_LIMIT_BYTES = 58 * 1024 * 1024

F32 = jnp.float32
BF16 = jnp.bfloat16


def _residual_norm(x, h, g, b):
    z = x + h
    mu = jnp.mean(z, axis=-1, keepdims=True)
    d = z - mu
    var = jnp.mean(d * d, axis=-1, keepdims=True)
    return d * lax.rsqrt(var + LN_EPS / ALPHA ** 2) * g + b


def _dot(a, b):
    return jnp.dot(a, b, preferred_element_type=F32)


def _ffn_ln_kernel(x_ref, xnext_ref, wu_ref, wd_ref, g_ref, b_ref, o_ref, act_scr):
    half = x_ref.shape[0] // 2

    def gated_up(x):
        xb = x.astype(BF16)
        a = _dot(xb, wu_ref[:, :D_FF])
        b = _dot(xb, wu_ref[:, D_FF:])
        return (a * jax.nn.sigmoid(a) * b).astype(BF16)

    def down(act):
        return _dot(act, wd_ref[...])

    @pl.when(pl.program_id(0) == 0)
    def _():
        act_scr[...] = gated_up(x_ref[:half, :])

    x0 = x_ref[:half, :]
    x1 = x_ref[half:, :]
    y0 = down(act_scr[...])
    y1 = down(gated_up(x1))
    act_scr[...] = gated_up(xnext_ref[...])
    o_ref[:half, :] = _residual_norm(x0, y0, g_ref[...], b_ref[...])
    o_ref[half:, :] = _residual_norm(x1, y1, g_ref[...], b_ref[...])


def _ffn_ln_single_kernel(x_ref, wu_ref, wd_ref, g_ref, b_ref, o_ref):
    x = x_ref[...]
    xb = x.astype(BF16)
    a = _dot(xb, wu_ref[:, :D_FF])
    b = _dot(xb, wu_ref[:, D_FF:])
    y = _dot((a * jax.nn.sigmoid(a) * b).astype(BF16), wd_ref[...])
    o_ref[...] = _residual_norm(x, y, g_ref[...], b_ref[...])


def _const_spec(shape):
    zeros = (0,) * len(shape)
    return pl.BlockSpec(shape, lambda *_: zeros, pipeline_mode=pl.Buffered(1))


def _ffn_ln(x2d, w_up, w_down, g, b, rows):
    n = x2d.shape[0]
    assert n % rows == 0 and rows % 2 == 0
    n_tiles = n // rows
    half = rows // 2
    weight_specs = [
        _const_spec((D_MODEL, 2 * D_FF)),
        _const_spec((D_FF, D_MODEL)),
        _const_spec((1, D_MODEL)),
        _const_spec((1, D_MODEL)),
    ]
    params = pltpu.CompilerParams(
        dimension_semantics=("arbitrary",), vmem_limit_bytes=VMEM_LIMIT_BYTES)
    if n_tiles == 1:
        return pl.pallas_call(
            _ffn_ln_single_kernel,
            grid=(1,),
            in_specs=[pl.BlockSpec((rows, D_MODEL), lambda i: (0, 0))] + weight_specs,
            out_specs=pl.BlockSpec((rows, D_MODEL), lambda i: (0, 0)),
            out_shape=jax.ShapeDtypeStruct((n, D_MODEL), F32),
            compiler_params=params,
            name="ffn_ln_single",
        )(x2d, w_up, w_down, g, b)
    return pl.pallas_call(
        _ffn_ln_kernel,
        grid=(n_tiles,),
        in_specs=[
            pl.BlockSpec((rows, D_MODEL), lambda i: (i, 0)),
            pl.BlockSpec((half, D_MODEL), lambda i: (2 * jnp.minimum(i + 1, n_tiles - 1), 0)),
        ] + weight_specs,
        out_specs=pl.BlockSpec((rows, D_MODEL), lambda i: (i, 0)),
        out_shape=jax.ShapeDtypeStruct((n, D_MODEL), F32),
        scratch_shapes=[pltpu.VMEM((half, D_FF), BF16)],
        compiler_params=params,
        name="ffn_ln",
    )(x2d, x2d, w_up, w_down, g, b)


def _mixer_kernel(*refs, T, pos0, zero_init):
    if zero_init:
        (x_ref, xnext_ref, wa_ref, wb_ref, convw_ref, g_ref, b_ref,
         y_ref, cbuf_ref, s_ref, rope_scr, dec_scr, rowdec_scr, qk_scr) = refs
    else:
        (x_ref, xnext_ref, wa_ref, wb_ref, convw_ref, g_ref, b_ref,
         cbuf0_ref, s0_ref,
         y_ref, cbuf_ref, s_ref, rope_scr, dec_scr, rowdec_scr, qk_scr) = refs
    t = pl.program_id(1)

    @pl.when(t == 0)
    def _():
        if zero_init:
            cbuf_ref[...] = jnp.zeros_like(cbuf_ref)
            s_ref[...] = jnp.zeros_like(s_ref)
        else:
            cbuf_ref[...] = cbuf0_ref[...]
            s_ref[...] = s0_ref[...]

    half = DK // 2
    b = pl.program_id(0)

    @pl.when(b == 0)
    def _():
        frame = lax.broadcasted_iota(jnp.int32, (T, half), 0)
        pos = (frame + (pos0 + t * T)).astype(F32)
        lane = lax.broadcasted_iota(jnp.int32, (T, half), 1).astype(F32)
        ang = pos * jnp.exp(lane * (-math.log(ROPE_BASE) / half))
        rows = pl.ds(pl.multiple_of(t * T, T), T)
        rope_scr[0, rows, :] = jnp.cos(ang)
        rope_scr[1, rows, :] = jnp.sin(ang)
        rope_scr[2, rows, :] = jnp.cos(ang) * K_SCALE
        rope_scr[3, rows, :] = jnp.sin(ang) * K_SCALE

    @pl.when((b == 0) & (t == 0))
    def _():
        li = lax.broadcasted_iota(jnp.int32, (T, T), 0)
        mi = lax.broadcasted_iota(jnp.int32, (T, T), 1)
        diff = (li - mi).astype(F32)
        idx = lax.broadcasted_iota(jnp.int32, (T, LANES), 0).astype(F32)
        for h in range(N_HEADS):
            lg = LOG_GAMMA[h]
            dec_scr[h] = jnp.where(diff >= 0, jnp.exp(lg * jnp.maximum(diff, 0.0)), 0.0)
            rowdec_scr[0, h] = jnp.exp(lg * (idx + 1.0))
            rowdec_scr[1, h] = jnp.exp(lg * (T - 1.0 - idx))

    x = x_ref[0]
    xb = x.astype(BF16)

    rows = pl.ds(pl.multiple_of(t * T, T), T)

    def weight(lo, width):
        if lo + width <= W_SPLIT:
            return wa_ref[:, lo:lo + width]
        if lo >= OFF_WRB:
            return wb_ref[:, lo - OFF_WRB + D_IN - W_SPLIT:lo - OFF_WRB + D_IN - W_SPLIT + width]
        if lo >= OFF_WCB:
            return wa_ref[:, lo - OFF_WCB + W_SPLIT:lo - OFF_WCB + W_SPLIT + width]
        assert lo >= W_SPLIT
        return wb_ref[:, lo - W_SPLIT:lo - W_SPLIT + width]

    def proj(lo, width, lhs=xb):
        return _dot(lhs, weight(lo, width))

    @pl.when((b == 0) & (t == 0))
    def _():
        qk_scr[...] = proj(OFF_Q, 2 * D_QK)

    def rotary(a, plane):
        cos, sin = rope_scr[plane, rows, :], rope_scr[plane + 1, rows, :]
        a1, a2 = a[:, :half], a[:, half:]
        return jnp.concatenate([a1 * cos - a2 * sin, a1 * sin + a2 * cos], axis=-1)

    def lane_tiled(a, width):
        return jnp.concatenate([a] * (width // LANES), axis=-1)

    def conv_input():
        u = proj(OFF_CC, D_CONV) * proj(OFF_CH, D_CONV)
        prev = cbuf_ref[0]
        row = lax.broadcasted_iota(jnp.int32, (T, D_CONV), 0)
        u1 = jnp.where(row == 0, prev[1:2], pltpu.roll(u, 1, 0))
        u2 = jnp.where(row == 0, prev[0:1], jnp.where(row == 1, prev[1:2], pltpu.roll(u, 2, 0)))
        cw = convw_ref[...]
        cbuf_ref[0] = u[T - 2:T]
        return u2 * cw[0:1] + u1 * cw[1:2] + u * cw[2:3]

    def retention_scores(h, q, k):
        qh = rotary(q[:, h * DK:(h + 1) * DK], 0)
        kh = rotary(k[:, h * DK:(h + 1) * DK], 2)
        qb = qh.astype(BF16)
        scores = lax.dot_general(qb, kh.astype(BF16), (((1,), (1,)), ((), ())),
                                 preferred_element_type=F32) * dec_scr[h]
        return qh, kh, qb, scores.astype(BF16)

    def retention_output(h, qh, kh, qb, scores, v):
        lg = LOG_GAMMA[h]
        vb = v[h].astype(BF16)
        s_old = s_ref[0, h]
        if T % LANES == 0:
            qc = (qh * lane_tiled(rowdec_scr[0, h], DK)).astype(BF16)
            o = _dot(jnp.concatenate([scores, qc], axis=-1),
                     jnp.concatenate([vb, s_old.astype(BF16)], axis=0))
        else:
            intra = _dot(scores, vb)
            cross = _dot(qb, s_old.astype(BF16)) * lane_tiled(rowdec_scr[0, h], DV)
            o = intra + cross
        kdec = (kh * lane_tiled(rowdec_scr[1, h], DK)).astype(BF16)
        s_ref[0, h] = math.exp(lg * T) * s_old + lax.dot_general(
            kdec, vb, (((0,), (0,)), ((), ())), preferred_element_type=F32)
        return o

    def normed_gated(o, gh):
        mu = jnp.mean(o, axis=-1, keepdims=True)
        d = o - mu
        var = jnp.mean(d * d, axis=-1, keepdims=True)
        return (d * lax.rsqrt(var + GN_EPS) * (gh * jax.nn.sigmoid(gh))).astype(BF16)

    q = qk_scr[:, :D_QK]
    k = qk_scr[:, D_QK:]
    v = [proj(OFF_V + h * DV, DV) for h in range(N_HEADS)]
    gated = []
    yconv = z = m_conv = gate_r = None
    for h in range(N_HEADS):
        head = retention_scores(h, q, k)
        gh = proj(OFF_G + h * DV, DV)
        o = retention_output(h, *head, v)
        if h == 0:
            yconv = conv_input()
        elif h == 1:
            z = (proj(OFF_CB, D_CONV) * yconv).astype(BF16)
        elif h == 2:
            m_conv = jax.nn.sigmoid(proj(OFF_GC, D_MODEL)) * _dot(z, weight(OFF_WCB, D_MODEL))
        else:
            gate_r = jax.nn.sigmoid(proj(OFF_GR, D_MODEL))
        gated.append(normed_gated(o, gh))
    y_ret = (_dot(jnp.concatenate(gated[:2], axis=-1), weight(OFF_WRB, D_MODEL))
             + _dot(jnp.concatenate(gated[2:], axis=-1), weight(OFF_WRB + D_MODEL, D_MODEL)))

    m = m_conv + gate_r * y_ret
    hmix = _dot(m.astype(BF16), weight(OFF_WOUT, D_MODEL))
    qk_scr[...] = proj(OFF_Q, 2 * D_QK, xnext_ref[0].astype(BF16))
    y_ref[0] = _residual_norm(x, hmix, g_ref[...], b_ref[...])


def _mixer(x, slab_a, slab_b, conv_w, g, b, T, pos0, init=None):
    B, seq, _ = x.shape
    assert seq % T == 0
    zero_init = init is None
    n_t = seq // T

    def next_step(i, t):
        lin = jnp.minimum(i * n_t + t + 1, B * n_t - 1)
        return (lin // n_t, lin % n_t, 0)

    in_specs = [
        pl.BlockSpec((1, T, D_MODEL), lambda i, t: (i, t, 0)),
        pl.BlockSpec((1, T, D_MODEL), next_step),
        _const_spec((D_MODEL, SLAB_COLS)),
        _const_spec((D_MODEL, SLAB_COLS)),
        _const_spec((CONV_W, D_CONV)),
        _const_spec((1, D_MODEL)),
        _const_spec((1, D_MODEL)),
    ]
    args = [x, x, slab_a, slab_b, conv_w, g, b]
    cbuf_spec = pl.BlockSpec((1, CONV_W - 1, D_CONV), lambda i, t: (i, 0, 0))
    s_spec = pl.BlockSpec((1, N_HEADS, DK, DV), lambda i, t: (i, 0, 0, 0))
    if not zero_init:
        in_specs += [cbuf_spec, s_spec]
        args += list(init)
    return pl.pallas_call(
        functools.partial(_mixer_kernel, T=T, pos0=pos0, zero_init=zero_init),
        grid=(B, seq // T),
        in_specs=in_specs,
        out_specs=[pl.BlockSpec((1, T, D_MODEL), lambda i, t: (i, t, 0)), cbuf_spec, s_spec],
        out_shape=[
            jax.ShapeDtypeStruct((B, seq, D_MODEL), F32),
            jax.ShapeDtypeStruct((B, CONV_W - 1, D_CONV), F32),
            jax.ShapeDtypeStruct((B, N_HEADS, DK, DV), F32),
        ],
        scratch_shapes=[
            pltpu.VMEM((4, seq, DK // 2), F32),
            pltpu.VMEM((N_HEADS, T, T), F32),
            pltpu.VMEM((2, N_HEADS, T, LANES), F32),
            pltpu.VMEM((T, 2 * D_QK), F32),
        ],
        compiler_params=pltpu.CompilerParams(
            dimension_semantics=("arbitrary", "arbitrary"), vmem_limit_bytes=VMEM_LIMIT_BYTES),
        name="mixer_prompt" if zero_init else "mixer_sample",
    )(*args)


def kernel(x_prompt, x_sample, cache_conv, state_ret, ln1_g, ln1_b, ffn1_w_up, ffn1_w_down, w_in,
           conv_w, w_conv_br, w_ret_br, w_out, ln2_g, ln2_b, ffn2_w_up, ffn2_w_down, ln3_g, ln3_b):
    assert w_in.shape[0] == DEPTH == 1
    Bp, Tp, _ = x_prompt.shape
    Bs, Ts, _ = x_sample.shape

    def cast(w, scale=1.0):
        return (w * scale).astype(BF16)

    wu1, wd1 = cast(ffn1_w_up[0]), cast(ffn1_w_down[0], 0.5 / ALPHA)
    wu2, wd2 = cast(ffn2_w_up[0]), cast(ffn2_w_down[0], 0.5 / ALPHA)
    slab_a = jnp.concatenate(
        [cast(w_in[0, :, :W_SPLIT]), cast(w_conv_br[0]), cast(w_out[0], 1.0 / ALPHA)], axis=1)
    slab_b = jnp.concatenate(
        [cast(w_in[0, :, W_SPLIT:]), cast(w_ret_br[0, :D_MODEL]), cast(w_ret_br[0, D_MODEL:])], axis=1)

    def layer(x, T, pos0, rows, init):
        B, seq, _ = x.shape
        h = _ffn_ln(x.reshape(B * seq, D_MODEL), wu1, wd1, ln1_g, ln1_b, rows)
        h, cbuf, s = _mixer(h.reshape(B, seq, D_MODEL), slab_a, slab_b, conv_w[0], ln2_g, ln2_b,
                            T, pos0, init)
        h = _ffn_ln(h.reshape(B * seq, D_MODEL), wu2, wd2, ln3_g, ln3_b, rows)
        return h.reshape(B, seq, D_MODEL), cbuf[None], s[None]

    yp, conv_p, ret_p = layer(x_prompt, MIXER_ROWS, 0, FFN_ROWS, None)
    ys, conv_s, ret_s = layer(x_sample, Ts, PAST_LEN, Bs * Ts, (cache_conv[0], state_ret[0]))
    return (yp, ys, conv_p, ret_p, conv_s, ret_s)
```

```python
import functools
import math

import jax
import jax.numpy as jnp
from jax import lax
from jax.experimental import pallas as pl
from jax.experimental.pallas import tpu as pltpu

D_MODEL = 1024
D_FF = 2816
D_CONV = 1024
CONV_W = 3
N_HEADS = 4
DK = 256
DV = 512
D_QK = N_HEADS * DK
D_V = N_HEADS * DV
PAST_LEN = 2048
ROPE_BASE = 10000.0
LN_EPS = 1e-5
GN_EPS = 1e-6
DEPTH = 1
ALPHA = (2 * DEPTH) ** 0.25
K_SCALE = DK ** -0.5
assert math.frexp(K_SCALE)[0] == 0.5, "the key scale is folded into the rotary tables"

OFF_CB = 0
OFF_CC = OFF_CB + D_CONV
OFF_CH = OFF_CC + D_CONV
OFF_Q = OFF_CH + D_CONV
OFF_K = OFF_Q + D_QK
OFF_V = OFF_K + D_QK
OFF_G = OFF_V + D_V
OFF_GC = OFF_G + D_V
OFF_GR = OFF_GC + D_MODEL
D_IN = OFF_GR + D_MODEL

LOG_GAMMA = tuple(math.log(1.0 - 2.0 ** (-5.0 - h)) for h in range(N_HEADS))

LANES = 128
W_SPLIT = D_IN // 2
SLAB_COLS = W_SPLIT + 2 * D_MODEL
assert (SLAB_COLS // LANES) % 8 != 0 and D_V == 2 * D_MODEL
OFF_WCB = D_IN
OFF_WOUT = OFF_WCB + D_MODEL
OFF_WRB = OFF_WOUT + D_MODEL
PACK_ROWS = 256
FFN_ROWS = 1024
MIXER_ROWS = 256
VMEM_LIMIT_BYTES = 58 * 1024 * 1024

F32 = jnp.float32
BF16 = jnp.bfloat16


def _residual_norm(x, h, g, b):
    z = x + h
    mu = jnp.mean(z, axis=-1, keepdims=True)
    d = z - mu
    var = jnp.mean(d * d, axis=-1, keepdims=True)
    return d * lax.rsqrt(var + LN_EPS / ALPHA ** 2) * g + b


def _dot(a, b):
    return jnp.dot(a, b, preferred_element_type=F32)


def _ffn_ln_kernel(x_ref, xnext_ref, wu_ref, wd_ref, g_ref, b_ref, o_ref, act_scr):
    half = x_ref.shape[0] // 2

    def gated_up(x):
        xb = x.astype(BF16)
        a = _dot(xb, wu_ref[:, :D_FF])
        b = _dot(xb, wu_ref[:, D_FF:])
        return (a * jax.nn.sigmoid(a) * b).astype(BF16)

    def down(act):
        return _dot(act, wd_ref[...])

    @pl.when(pl.program_id(0) == 0)
    def _():
        act_scr[...] = gated_up(x_ref[:half, :])

    x0 = x_ref[:half, :]
    x1 = x_ref[half:, :]
    y0 = down(act_scr[...])
    y1 = down(gated_up(x1))
    act_scr[...] = gated_up(xnext_ref[...])
    o_ref[:half, :] = _residual_norm(x0, y0, g_ref[...], b_ref[...])
    o_ref[half:, :] = _residual_norm(x1, y1, g_ref[...], b_ref[...])


def _pack_kernel(*refs, scales):
    *piece_refs, o_ref = refs
    col = 0
    for p_ref, scale in zip(piece_refs, scales):
        width = p_ref.shape[-1]
        piece = p_ref[...] if scale == 1.0 else p_ref[...] * scale
        o_ref[:, col:col + width] = piece.astype(BF16)
        col += width


def _pack_bf16(pieces, rows):
    assert rows % PACK_ROWS == 0
    in_specs, args = [], []
    for w, row0, col0, width, _ in pieces:
        assert row0 % PACK_ROWS == 0 and col0 % width == 0
        in_specs.append(pl.BlockSpec(
            (PACK_ROWS, width),
            functools.partial(lambda i, r, c: (i + r, c), r=row0 // PACK_ROWS, c=col0 // width)))
        args.append(w)
    total = sum(p[3] for p in pieces)
    return pl.pallas_call(
        functools.partial(_pack_kernel, scales=tuple(p[4] for p in pieces)),
        grid=(rows // PACK_ROWS,),
        in_specs=in_specs,
        out_specs=pl.BlockSpec((PACK_ROWS, total), lambda i: (i, 0)),
        out_shape=jax.ShapeDtypeStruct((rows, total), BF16),
        compiler_params=pltpu.CompilerParams(
            dimension_semantics=("arbitrary",), vmem_limit_bytes=VMEM_LIMIT_BYTES),
        name="pack_bf16",
    )(*args)


def _ffn_ln_single_kernel(x_ref, wu_ref, wd_ref, g_ref, b_ref, o_ref):
    x = x_ref[...]
    xb = x.astype(BF16)
    a = _dot(xb, wu_ref[:, :D_FF])
    b = _dot(xb, wu_ref[:, D_FF:])
    y = _dot((a * jax.nn.sigmoid(a) * b).astype(BF16), wd_ref[...])
    o_ref[...] = _residual_norm(x, y, g_ref[...], b_ref[...])


def _const_spec(shape):
    zeros = (0,) * len(shape)
    return pl.BlockSpec(shape, lambda *_: zeros, pipeline_mode=pl.Buffered(1))


def _ffn_ln(x2d, w_up, w_down, g, b, rows):
    n = x2d.shape[0]
    assert n % rows == 0 and rows % 2 == 0
    n_tiles = n // rows
    half = rows // 2
    weight_specs = [
        _const_spec((D_MODEL, 2 * D_FF)),
        _const_spec((D_FF, D_MODEL)),
        _const_spec((1, D_MODEL)),
        _const_spec((1, D_MODEL)),
    ]
    params = pltpu.CompilerParams(
        dimension_semantics=("arbitrary",), vmem_limit_bytes=VMEM_LIMIT_BYTES)
    if n_tiles == 1:
        return pl.pallas_call(
            _ffn_ln_single_kernel,
            grid=(1,),
            in_specs=[pl.BlockSpec((rows, D_MODEL), lambda i: (0, 0))] + weight_specs,
            out_specs=pl.BlockSpec((rows, D_MODEL), lambda i: (0, 0)),
            out_shape=jax.ShapeDtypeStruct((n, D_MODEL), F32),
            compiler_params=params,
            name="ffn_ln_single",
        )(x2d, w_up, w_down, g, b)
    return pl.pallas_call(
        _ffn_ln_kernel,
        grid=(n_tiles,),
        in_specs=[
            pl.BlockSpec((rows, D_MODEL), lambda i: (i, 0)),
            pl.BlockSpec((half, D_MODEL), lambda i: (2 * jnp.minimum(i + 1, n_tiles - 1), 0)),
        ] + weight_specs,
        out_specs=pl.BlockSpec((rows, D_MODEL), lambda i: (i, 0)),
        out_shape=jax.ShapeDtypeStruct((n, D_MODEL), F32),
        scratch_shapes=[pltpu.VMEM((half, D_FF), BF16)],
        compiler_params=params,
        name="ffn_ln",
    )(x2d, x2d, w_up, w_down, g, b)


def _mixer_kernel(*refs, T, pos0, zero_init):
    if zero_init:
        (x_ref, xnext_ref, wa_ref, wb_ref, convw_ref, g_ref, b_ref,
         y_ref, cbuf_ref, s_ref, rope_scr, dec_scr, rowdec_scr, qk_scr) = refs
    else:
        (x_ref, xnext_ref, wa_ref, wb_ref, convw_ref, g_ref, b_ref,
         cbuf0_ref, s0_ref,
         y_ref, cbuf_ref, s_ref, rope_scr, dec_scr, rowdec_scr, qk_scr) = refs
    t = pl.program_id(1)

    @pl.when(t == 0)
    def _():
        if zero_init:
            cbuf_ref[...] = jnp.zeros_like(cbuf_ref)
            s_ref[...] = jnp.zeros_like(s_ref)
        else:
            cbuf_ref[...] = cbuf0_ref[...]
            s_ref[...] = s0_ref[...]

    half = DK // 2
    b = pl.program_id(0)

    @pl.when(b == 0)
    def _():
        frame = lax.broadcasted_iota(jnp.int32, (T, half), 0)
        pos = (frame + (pos0 + t * T)).astype(F32)
        lane = lax.broadcasted_iota(jnp.int32, (T, half), 1).astype(F32)
        ang = pos * jnp.exp(lane * (-math.log(ROPE_BASE) / half))
        rows = pl.ds(pl.multiple_of(t * T, T), T)
        rope_scr[0, rows, :] = jnp.cos(ang)
        rope_scr[1, rows, :] = jnp.sin(ang)
        rope_scr[2, rows, :] = jnp.cos(ang) * K_SCALE
        rope_scr[3, rows, :] = jnp.sin(ang) * K_SCALE

    @pl.when((b == 0) & (t == 0))
    def _():
        li = lax.broadcasted_iota(jnp.int32, (T, T), 0)
        mi = lax.broadcasted_iota(jnp.int32, (T, T), 1)
        diff = (li - mi).astype(F32)
        idx = lax.broadcasted_iota(jnp.int32, (T, LANES), 0).astype(F32)
        for h in range(N_HEADS):
            lg = LOG_GAMMA[h]
            dec_scr[h] = jnp.where(diff >= 0, jnp.exp(lg * jnp.maximum(diff, 0.0)), 0.0)
            rowdec_scr[0, h] = jnp.exp(lg * (idx + 1.0))
            rowdec_scr[1, h] = jnp.exp(lg * (T - 1.0 - idx))

    x = x_ref[0]
    xb = x.astype(BF16)

    rows = pl.ds(pl.multiple_of(t * T, T), T)

    def weight(lo, width):
        if lo + width <= W_SPLIT:
            return wa_ref[:, lo:lo + width]
        if lo >= OFF_WRB:
            return wb_ref[:, lo - OFF_WRB + D_IN - W_SPLIT:lo - OFF_WRB + D_IN - W_SPLIT + width]
        if lo >= OFF_WCB:
            return wa_ref[:, lo - OFF_WCB + W_SPLIT:lo - OFF_WCB + W_SPLIT + width]
        assert lo >= W_SPLIT
        return wb_ref[:, lo - W_SPLIT:lo - W_SPLIT + width]

    def proj(lo, width, lhs=xb):
        return _dot(lhs, weight(lo, width))

    @pl.when((b == 0) & (t == 0))
    def _():
        qk_scr[...] = proj(OFF_Q, 2 * D_QK)

    def rotary(a, plane):
        cos, sin = rope_scr[plane, rows, :], rope_scr[plane + 1, rows, :]
        a1, a2 = a[:, :half], a[:, half:]
        return jnp.concatenate([a1 * cos - a2 * sin, a1 * sin + a2 * cos], axis=-1)

    def lane_tiled(a, width):
        return jnp.concatenate([a] * (width // LANES), axis=-1)

    def conv_input():
        u = proj(OFF_CC, D_CONV) * proj(OFF_CH, D_CONV)
        prev = cbuf_ref[0]
        row = lax.broadcasted_iota(jnp.int32, (T, D_CONV), 0)
        u1 = jnp.where(row == 0, prev[1:2], pltpu.roll(u, 1, 0))
        u2 = jnp.where(row == 0, prev[0:1], jnp.where(row == 1, prev[1:2], pltpu.roll(u, 2, 0)))
        cw = convw_ref[...]
        cbuf_ref[0] = u[T - 2:T]
        return u2 * cw[0:1] + u1 * cw[1:2] + u * cw[2:3]

    def retention_scores(h, q, k):
        qh = rotary(q[:, h * DK:(h + 1) * DK], 0)
        kh = rotary(k[:, h * DK:(h + 1) * DK], 2)
        qb = qh.astype(BF16)
        scores = lax.dot_general(qb, kh.astype(BF16), (((1,), (1,)), ((), ())),
                                 preferred_element_type=F32) * dec_scr[h]
        return qh, kh, qb, scores.astype(BF16)

    def retention_output(h, qh, kh, qb, scores, v):
        lg = LOG_GAMMA[h]
        vb = v[h].astype(BF16)
        s_old = s_ref[0, h]
        if T % LANES == 0:
            qc = (qh * lane_tiled(rowdec_scr[0, h], DK)).astype(BF16)
            o = _dot(jnp.concatenate([scores, qc], axis=-1),
                     jnp.concatenate([vb, s_old.astype(BF16)], axis=0))
        else:
            intra = _dot(scores, vb)
            cross = _dot(qb, s_old.astype(BF16)) * lane_tiled(rowdec_scr[0, h], DV)
            o = intra + cross
        kdec = (kh * lane_tiled(rowdec_scr[1, h], DK)).astype(BF16)
        s_ref[0, h] = math.exp(lg * T) * s_old + lax.dot_general(
            kdec, vb, (((0,), (0,)), ((), ())), preferred_element_type=F32)
        return o

    def normed_gated(o, gh):
        mu = jnp.mean(o, axis=-1, keepdims=True)
        d = o - mu
        var = jnp.mean(d * d, axis=-1, keepdims=True)
        return (d * lax.rsqrt(var + GN_EPS) * (gh * jax.nn.sigmoid(gh))).astype(BF16)

    q = qk_scr[:, :D_QK]
    k = qk_scr[:, D_QK:]
    v = [proj(OFF_V + h * DV, DV) for h in range(N_HEADS)]
    gated = []
    yconv = z = m_conv = gate_r = None
    for h in range(N_HEADS):
        head = retention_scores(h, q, k)
        gh = proj(OFF_G + h * DV, DV)
        o = retention_output(h, *head, v)
        if h == 0:
            yconv = conv_input()
        elif h == 1:
            z = (proj(OFF_CB, D_CONV) * yconv).astype(BF16)
        elif h == 2:
            m_conv = jax.nn.sigmoid(proj(OFF_GC, D_MODEL)) * _dot(z, weight(OFF_WCB, D_MODEL))
        else:
            gate_r = jax.nn.sigmoid(proj(OFF_GR, D_MODEL))
        gated.append(normed_gated(o, gh))
    y_ret = (_dot(jnp.concatenate(gated[:2], axis=-1), weight(OFF_WRB, D_MODEL))
             + _dot(jnp.concatenate(gated[2:], axis=-1), weight(OFF_WRB + D_MODEL, D_MODEL)))

    m = m_conv + gate_r * y_ret
    hmix = _dot(m.astype(BF16), weight(OFF_WOUT, D_MODEL))
    qk_scr[...] = proj(OFF_Q, 2 * D_QK, xnext_ref[0].astype(BF16))
    y_ref[0] = _residual_norm(x, hmix, g_ref[...], b_ref[...])


def _mixer(x, slab_a, slab_b, conv_w, g, b, T, pos0, init=None):
    B, seq, _ = x.shape
    assert seq % T == 0
    zero_init = init is None
    n_t = seq // T

    def next_step(i, t):
        lin = jnp.minimum(i * n_t + t + 1, B * n_t - 1)
        return (lin // n_t, lin % n_t, 0)

    in_specs = [
        pl.BlockSpec((1, T, D_MODEL), lambda i, t: (i, t, 0)),
        pl.BlockSpec((1, T, D_MODEL), next_step),
        _const_spec((D_MODEL, SLAB_COLS)),
        _const_spec((D_MODEL, SLAB_COLS)),
        _const_spec((CONV_W, D_CONV)),
        _const_spec((1, D_MODEL)),
        _const_spec((1, D_MODEL)),
    ]
    args = [x, x, slab_a, slab_b, conv_w, g, b]
    cbuf_spec = pl.BlockSpec((1, CONV_W - 1, D_CONV), lambda i, t: (i, 0, 0))
    s_spec = pl.BlockSpec((1, N_HEADS, DK, DV), lambda i, t: (i, 0, 0, 0))
    if not zero_init:
        in_specs += [cbuf_spec, s_spec]
        args += list(init)
    return pl.pallas_call(
        functools.partial(_mixer_kernel, T=T, pos0=pos0, zero_init=zero_init),
        grid=(B, seq // T),
        in_specs=in_specs,
        out_specs=[pl.BlockSpec((1, T, D_MODEL), lambda i, t: (i, t, 0)), cbuf_spec, s_spec],
        out_shape=[
            jax.ShapeDtypeStruct((B, seq, D_MODEL), F32),
            jax.ShapeDtypeStruct((B, CONV_W - 1, D_CONV), F32),
            jax.ShapeDtypeStruct((B, N_HEADS, DK, DV), F32),
        ],
        scratch_shapes=[
            pltpu.VMEM((4, seq, DK // 2), F32),
            pltpu.VMEM((N_HEADS, T, T), F32),
            pltpu.VMEM((2, N_HEADS, T, LANES), F32),
            pltpu.VMEM((T, 2 * D_QK), F32),
        ],
        compiler_params=pltpu.CompilerParams(
            dimension_semantics=("arbitrary", "arbitrary"), vmem_limit_bytes=VMEM_LIMIT_BYTES),
        name="mixer_prompt" if zero_init else "mixer_sample",
    )(*args)


def kernel(x_prompt, x_sample, cache_conv, state_ret, ln1_g, ln1_b, ffn1_w_up, ffn1_w_down, w_in,
           conv_w, w_conv_br, w_ret_br, w_out, ln2_g, ln2_b, ffn2_w_up, ffn2_w_down, ln3_g, ln3_b):
    assert w_in.shape[0] == DEPTH == 1
    Bp, Tp, _ = x_prompt.shape
    Bs, Ts, _ = x_sample.shape

    def cast(w, scale=1.0):
        rows, cols = w.shape[1:]
        return _pack_bf16([(w[0], 0, 0, cols, scale)], rows)

    wu1, wd1 = cast(ffn1_w_up), cast(ffn1_w_down, 0.5 / ALPHA)
    wu2, wd2 = cast(ffn2_w_up), cast(ffn2_w_down, 0.5 / ALPHA)
    slab_a = _pack_bf16([(w_in[0], 0, 0, W_SPLIT, 1.0), (w_conv_br[0], 0, 0, D_MODEL, 1.0),
                         (w_out[0], 0, 0, D_MODEL, 1.0 / ALPHA)], D_MODEL)
    slab_b = _pack_bf16([(w_in[0], 0, W_SPLIT, D_IN - W_SPLIT, 1.0),
                         (w_ret_br[0], 0, 0, D_MODEL, 1.0),
                         (w_ret_br[0], D_MODEL, 0, D_MODEL, 1.0)], D_MODEL)

    def layer(x, T, pos0, rows, init):
        B, seq, _ = x.shape
        h = _ffn_ln(x.reshape(B * seq, D_MODEL), wu1, wd1, ln1_g, ln1_b, rows)
        h, cbuf, s = _mixer(h.reshape(B, seq, D_MODEL), slab_a, slab_b, conv_w[0], ln2_g, ln2_b,
                            T, pos0, init)
        h = _ffn_ln(h.reshape(B * seq, D_MODEL), wu2, wd2, ln3_g, ln3_b, rows)
        return h.reshape(B, seq, D_MODEL), cbuf[None], s[None]

    yp, conv_p, ret_p = layer(x_prompt, MIXER_ROWS, 0, FFN_ROWS, None)
    ys, conv_s, ret_s = layer(x_sample, Ts, PAST_LEN, Bs * Ts, (cache_conv[0], state_ret[0]))
    return (yp, ys, conv_p, ret_p, conv_s, ret_s)
```

```python
import functools
import math

import jax
import jax.numpy as jnp
from jax import lax
from jax.experimental import pallas as pl
from jax.experimental.pallas import tpu as pltpu

D_MODEL = 1024
D_FF = 2816
D_CONV = 1024
CONV_W = 3
N_HEADS = 4
DK = 256
DV = 512
D_QK = N_HEADS * DK
D_V = N_HEADS * DV
PAST_LEN = 2048
ROPE_BASE = 10000.0
LN_EPS = 1e-5
GN_EPS = 1e-6
DEPTH = 1
ALPHA = (2 * DEPTH) ** 0.25
K_SCALE = DK ** -0.5
assert math.frexp(K_SCALE)[0] == 0.5, "the key scale is folded into the rotary tables"

OFF_CB = 0
OFF_CC = OFF_CB + D_CONV
OFF_CH = OFF_CC + D_CONV
OFF_Q = OFF_CH + D_CONV
OFF_K = OFF_Q + D_QK
OFF_V = OFF_K + D_QK
OFF_G = OFF_V + D_V
OFF_GC = OFF_G + D_V
OFF_GR = OFF_GC + D_MODEL
D_IN = OFF_GR + D_MODEL

LOG_GAMMA = tuple(math.log(1.0 - 2.0 ** (-5.0 - h)) for h in range(N_HEADS))

LANES = 128
W_SPLIT = D_IN // 2
SLAB_COLS = W_SPLIT + 2 * D_MODEL
assert (SLAB_COLS // LANES) % 8 != 0 and D_V == 2 * D_MODEL
OFF_WCB = D_IN
OFF_WOUT = OFF_WCB + D_MODEL
OFF_WRB = OFF_WOUT + D_MODEL
PACK_STEPS = 4
FFN_ROWS = 1024
MIXER_ROWS = 256
VMEM_LIMIT_BYTES = 58 * 1024 * 1024

F32 = jnp.float32
BF16 = jnp.bfloat16


def _residual_norm(o_ref, rows, x, h, g_ref, b_ref):
    o_ref[rows] = x + h
    z, g, b = o_ref[rows], g_ref[...], b_ref[...]
    mu = jnp.mean(z, axis=-1, keepdims=True)
    d = z - mu
    var = jnp.mean(d * d, axis=-1, keepdims=True)
    o_ref[rows] = d * lax.rsqrt(var + LN_EPS / ALPHA ** 2) * g + b


def _dot(a, b):
    return jnp.dot(a, b, preferred_element_type=F32)


def _ffn_ln_kernel(x_ref, xnext_ref, wu_ref, wd_ref, g_ref, b_ref, o_ref, act_scr):
    half = x_ref.shape[0] // 2

    def gated_up(x):
        xb = x.astype(BF16)
        a = _dot(xb, wu_ref[:, :D_FF])
        b = _dot(xb, wu_ref[:, D_FF:])
        return (a * jax.nn.sigmoid(a) * b).astype(BF16)

    def down(act):
        return _dot(act, wd_ref[...])

    @pl.when(pl.program_id(0) == 0)
    def _():
        act_scr[...] = gated_up(x_ref[:half, :])

    x0 = x_ref[:half, :]
    x1 = x_ref[half:, :]
    y0 = down(act_scr[...])
    y1 = down(gated_up(x1))
    act_scr[...] = gated_up(xnext_ref[...])
    _residual_norm(o_ref, pl.ds(0, half), x0, y0, g_ref, b_ref)
    _residual_norm(o_ref, pl.ds(half, half), x1, y1, g_ref, b_ref)


def _pack_kernel(*refs, scales):
    *piece_refs, o_ref = refs
    col = 0
    for p_ref, scale in zip(piece_refs, scales):
        width = p_ref.shape[-1]
        piece = p_ref[...] if scale == 1.0 else p_ref[...] * scale
        o_ref[:, col:col + width] = piece.astype(BF16)
        col += width


def _pack_bf16(pieces, rows):
    assert rows % (PACK_STEPS * 16) == 0
    step_rows = rows // PACK_STEPS
    in_specs, args = [], []
    for w, row0, col0, width, _ in pieces:
        assert row0 % step_rows == 0 and col0 % width == 0
        in_specs.append(pl.BlockSpec(
            (step_rows, width),
            functools.partial(lambda i, r, c: (i + r, c), r=row0 // step_rows, c=col0 // width)))
        args.append(w)
    total = sum(p[3] for p in pieces)
    return pl.pallas_call(
        functools.partial(_pack_kernel, scales=tuple(p[4] for p in pieces)),
        grid=(PACK_STEPS,),
        in_specs=in_specs,
        out_specs=pl.BlockSpec((step_rows, total), lambda i: (i, 0)),
        out_shape=jax.ShapeDtypeStruct((rows, total), BF16),
        compiler_params=pltpu.CompilerParams(
            dimension_semantics=("arbitrary",), vmem_limit_bytes=VMEM_LIMIT_BYTES),
        name="pack_bf16",
    )(*args)


def _ffn_ln_single_kernel(x_ref, wu_ref, wd_ref, g_ref, b_ref, o_ref):
    x = x_ref[...]
    xb = x.astype(BF16)
    a = _dot(xb, wu_ref[:, :D_FF])
    b = _dot(xb, wu_ref[:, D_FF:])
    y = _dot((a * jax.nn.sigmoid(a) * b).astype(BF16), wd_ref[...])
    _residual_norm(o_ref, slice(None), x, y, g_ref, b_ref)


def _const_spec(shape):
    zeros = (0,) * len(shape)
    return pl.BlockSpec(shape, lambda *_: zeros, pipeline_mode=pl.Buffered(1))


def _ffn_ln(x2d, w_up, w_down, g, b, rows):
    n = x2d.shape[0]
    assert n % rows == 0 and rows % 2 == 0
    n_tiles = n // rows
    half = rows // 2
    weight_specs = [
        _const_spec((D_MODEL, 2 * D_FF)),
        _const_spec((D_FF, D_MODEL)),
        _const_spec((1, D_MODEL)),
        _const_spec((1, D_MODEL)),
    ]
    params = pltpu.CompilerParams(
        dimension_semantics=("arbitrary",), vmem_limit_bytes=VMEM_LIMIT_BYTES)
    if n_tiles == 1:
        return pl.pallas_call(
            _ffn_ln_single_kernel,
            grid=(1,),
            in_specs=[pl.BlockSpec((rows, D_MODEL), lambda i: (0, 0))] + weight_specs,
            out_specs=pl.BlockSpec((rows, D_MODEL), lambda i: (0, 0)),
            out_shape=jax.ShapeDtypeStruct((n, D_MODEL), F32),
            compiler_params=params,
            name="ffn_ln_single",
        )(x2d, w_up, w_down, g, b)
    return pl.pallas_call(
        _ffn_ln_kernel,
        grid=(n_tiles,),
        in_specs=[
            pl.BlockSpec((rows, D_MODEL), lambda i: (i, 0)),
            pl.BlockSpec((half, D_MODEL), lambda i: (2 * jnp.minimum(i + 1, n_tiles - 1), 0)),
        ] + weight_specs,
        out_specs=pl.BlockSpec((rows, D_MODEL), lambda i: (i, 0)),
        out_shape=jax.ShapeDtypeStruct((n, D_MODEL), F32),
        scratch_shapes=[pltpu.VMEM((half, D_FF), BF16)],
        compiler_params=params,
        name="ffn_ln",
    )(x2d, x2d, w_up, w_down, g, b)


def _mixer_kernel(*refs, T, pos0, zero_init):
    if zero_init:
        (x_ref, xnext_ref, wa_ref, wb_ref, convw_ref, g_ref, b_ref,
         y_ref, cbuf_ref, s_ref, rope_scr, dec_scr, rowdec_scr, qk_scr) = refs
    else:
        (x_ref, xnext_ref, wa_ref, wb_ref, convw_ref, g_ref, b_ref,
         cbuf0_ref, s0_ref,
         y_ref, cbuf_ref, s_ref, rope_scr, dec_scr, rowdec_scr, qk_scr) = refs
    t = pl.program_id(1)

    @pl.when(t == 0)
    def _():
        if zero_init:
            cbuf_ref[...] = jnp.zeros_like(cbuf_ref)
            s_ref[...] = jnp.zeros_like(s_ref)
        else:
            cbuf_ref[...] = cbuf0_ref[...]
            s_ref[...] = s0_ref[...]

    half = DK // 2
    b = pl.program_id(0)

    @pl.when(b == 0)
    def _():
        frame = lax.broadcasted_iota(jnp.int32, (T, half), 0)
        pos = (frame + (pos0 + t * T)).astype(F32)
        lane = lax.broadcasted_iota(jnp.int32, (T, half), 1).astype(F32)
        ang = pos * jnp.exp(lane * (-math.log(ROPE_BASE) / half))
        rows = pl.ds(pl.multiple_of(t * T, T), T)
        rope_scr[0, rows, :] = jnp.cos(ang)
        rope_scr[1, rows, :] = jnp.sin(ang)
        rope_scr[2, rows, :] = jnp.cos(ang) * K_SCALE
        rope_scr[3, rows, :] = jnp.sin(ang) * K_SCALE

    @pl.when((b == 0) & (t == 0))
    def _():
        li = lax.broadcasted_iota(jnp.int32, (T, T), 0)
        mi = lax.broadcasted_iota(jnp.int32, (T, T), 1)
        diff = (li - mi).astype(F32)
        idx = lax.broadcasted_iota(jnp.int32, (T, LANES), 0).astype(F32)
        for h in range(N_HEADS):
            lg = LOG_GAMMA[h]
            dec_scr[h] = jnp.where(diff >= 0, jnp.exp(lg * jnp.maximum(diff, 0.0)), 0.0)
            rowdec_scr[0, h] = jnp.exp(lg * (idx + 1.0))
            rowdec_scr[1, h] = jnp.exp(lg * (T - 1.0 - idx))

    x = x_ref[0]
    xb = x.astype(BF16)

    rows = pl.ds(pl.multiple_of(t * T, T), T)

    def weight(lo, width):
        if lo + width <= W_SPLIT:
            return wa_ref[:, lo:lo + width]
        if lo >= OFF_WRB:
            return wb_ref[:, lo - OFF_WRB + D_IN - W_SPLIT:lo - OFF_WRB + D_IN - W_SPLIT + width]
        if lo >= OFF_WCB:
            return wa_ref[:, lo - OFF_WCB + W_SPLIT:lo - OFF_WCB + W_SPLIT + width]
        assert lo >= W_SPLIT
        return wb_ref[:, lo - W_SPLIT:lo - W_SPLIT + width]

    def proj(lo, width, lhs=xb):
        return _dot(lhs, weight(lo, width))

    @pl.when((b == 0) & (t == 0))
    def _():
        qk_scr[...] = proj(OFF_Q, 2 * D_QK)

    def rotary(a, plane):
        cos, sin = rope_scr[plane, rows, :], rope_scr[plane + 1, rows, :]
        a1, a2 = a[:, :half], a[:, half:]
        return jnp.concatenate([a1 * cos - a2 * sin, a1 * sin + a2 * cos], axis=-1)

    def lane_tiled(a, width):
        return jnp.concatenate([a] * (width // LANES), axis=-1)

    def conv_input():
        u = proj(OFF_CC, D_CONV) * proj(OFF_CH, D_CONV)
        prev = cbuf_ref[0]
        row = lax.broadcasted_iota(jnp.int32, (T, D_CONV), 0)
        u1 = jnp.where(row == 0, prev[1:2], pltpu.roll(u, 1, 0))
        u2 = jnp.where(row == 0, prev[0:1], jnp.where(row == 1, prev[1:2], pltpu.roll(u, 2, 0)))
        cw = convw_ref[...]
        cbuf_ref[0] = u[T - 2:T]
        return u2 * cw[0:1] + u1 * cw[1:2] + u * cw[2:3]

    def retention_scores(h, q, k):
        qh = rotary(q[:, h * DK:(h + 1) * DK], 0)
        kh = rotary(k[:, h * DK:(h + 1) * DK], 2)
        qb = qh.astype(BF16)
        scores = lax.dot_general(qb, kh.astype(BF16), (((1,), (1,)), ((), ())),
                                 preferred_element_type=F32) * dec_scr[h]
        return qh, kh, qb, scores.astype(BF16)

    def retention_output(h, qh, kh, qb, scores, v):
        lg = LOG_GAMMA[h]
        vb = v[h].astype(BF16)
        s_old = s_ref[0, h]
        if T % LANES == 0:
            qc = (qh * lane_tiled(rowdec_scr[0, h], DK)).astype(BF16)
            o = _dot(jnp.concatenate([scores, qc], axis=-1),
                     jnp.concatenate([vb, s_old.astype(BF16)], axis=0))
        else:
            intra = _dot(scores, vb)
            cross = _dot(qb, s_old.astype(BF16)) * lane_tiled(rowdec_scr[0, h], DV)
            o = intra + cross
        kdec = (kh * lane_tiled(rowdec_scr[1, h], DK)).astype(BF16)
        s_ref[0, h] = math.exp(lg * T) * s_old + lax.dot_general(
            kdec, vb, (((0,), (0,)), ((), ())), preferred_element_type=F32)
        return o

    def normed_gated(o, gh):
        mu = jnp.mean(o, axis=-1, keepdims=True)
        d = o - mu
        var = jnp.mean(d * d, axis=-1, keepdims=True)
        return (d * lax.rsqrt(var + GN_EPS) * (gh * jax.nn.sigmoid(gh))).astype(BF16)

    q = qk_scr[:, :D_QK]
    k = qk_scr[:, D_QK:]
    v = [proj(OFF_V + h * DV, DV) for h in range(N_HEADS)]
    gated = []
    yconv = z = m_conv = gate_r = None
    for h in range(N_HEADS):
        head = retention_scores(h, q, k)
        gh = proj(OFF_G + h * DV, DV)
        o = retention_output(h, *head, v)
        if h == 0:
            yconv = conv_input()
        elif h == 1:
            z = (proj(OFF_CB, D_CONV) * yconv).astype(BF16)
        elif h == 2:
            m_conv = jax.nn.sigmoid(proj(OFF_GC, D_MODEL)) * _dot(z, weight(OFF_WCB, D_MODEL))
        else:
            gate_r = jax.nn.sigmoid(proj(OFF_GR, D_MODEL))
        gated.append(normed_gated(o, gh))
    y_ret = (_dot(jnp.concatenate(gated[:2], axis=-1), weight(OFF_WRB, D_MODEL))
             + _dot(jnp.concatenate(gated[2:], axis=-1), weight(OFF_WRB + D_MODEL, D_MODEL)))

    m = m_conv + gate_r * y_ret
    hmix = _dot(m.astype(BF16), weight(OFF_WOUT, D_MODEL))
    qk_scr[...] = proj(OFF_Q, 2 * D_QK, xnext_ref[0].astype(BF16))
    _residual_norm(y_ref, 0, x, hmix, g_ref, b_ref)


def _mixer(x, slab_a, slab_b, conv_w, g, b, T, pos0, init=None):
    B, seq, _ = x.shape
    assert seq % T == 0
    zero_init = init is None
    n_t = seq // T

    def next_step(i, t):
        lin = jnp.minimum(i * n_t + t + 1, B * n_t - 1)
        return (lin // n_t, lin % n_t, 0)

    in_specs = [
        pl.BlockSpec((1, T, D_MODEL), lambda i, t: (i, t, 0)),
        pl.BlockSpec((1, T, D_MODEL), next_step),
        _const_spec((D_MODEL, SLAB_COLS)),
        _const_spec((D_MODEL, SLAB_COLS)),
        _const_spec((CONV_W, D_CONV)),
        _const_spec((1, D_MODEL)),
        _const_spec((1, D_MODEL)),
    ]
    args = [x, x, slab_a, slab_b, conv_w, g, b]
    cbuf_spec = pl.BlockSpec((1, CONV_W - 1, D_CONV), lambda i, t: (i, 0, 0))
    s_spec = pl.BlockSpec((1, N_HEADS, DK, DV), lambda i, t: (i, 0, 0, 0))
    if not zero_init:
        in_specs += [cbuf_spec, s_spec]
        args += list(init)
    return pl.pallas_call(
        functools.partial(_mixer_kernel, T=T, pos0=pos0, zero_init=zero_init),
        grid=(B, seq // T),
        in_specs=in_specs,
        out_specs=[pl.BlockSpec((1, T, D_MODEL), lambda i, t: (i, t, 0)), cbuf_spec, s_spec],
        out_shape=[
            jax.ShapeDtypeStruct((B, seq, D_MODEL), F32),
            jax.ShapeDtypeStruct((B, CONV_W - 1, D_CONV), F32),
            jax.ShapeDtypeStruct((B, N_HEADS, DK, DV), F32),
        ],
        scratch_shapes=[
            pltpu.VMEM((4, seq, DK // 2), F32),
            pltpu.VMEM((N_HEADS, T, T), F32),
            pltpu.VMEM((2, N_HEADS, T, LANES), F32),
            pltpu.VMEM((T, 2 * D_QK), F32),
        ],
        compiler_params=pltpu.CompilerParams(
            dimension_semantics=("arbitrary", "arbitrary"), vmem_limit_bytes=VMEM_LIMIT_BYTES),
        name="mixer_prompt" if zero_init else "mixer_sample",
    )(*args)


def kernel(x_prompt, x_sample, cache_conv, state_ret, ln1_g, ln1_b, ffn1_w_up, ffn1_w_down, w_in,
           conv_w, w_conv_br, w_ret_br, w_out, ln2_g, ln2_b, ffn2_w_up, ffn2_w_down, ln3_g, ln3_b):
    assert w_in.shape[0] == DEPTH == 1
    Bp, Tp, _ = x_prompt.shape
    Bs, Ts, _ = x_sample.shape

    def cast(w, scale=1.0):
        rows, cols = w.shape[1:]
        return _pack_bf16([(w[0], 0, 0, cols, scale)], rows)

    wu1, wd1 = cast(ffn1_w_up), cast(ffn1_w_down, 0.5 / ALPHA)
    wu2, wd2 = cast(ffn2_w_up), cast(ffn2_w_down, 0.5 / ALPHA)
    slab_a = _pack_bf16([(w_in[0], 0, 0, W_SPLIT, 1.0), (w_conv_br[0], 0, 0, D_MODEL, 1.0),
                         (w_out[0], 0, 0, D_MODEL, 1.0 / ALPHA)], D_MODEL)
    slab_b = _pack_bf16([(w_in[0], 0, W_SPLIT, D_IN - W_SPLIT, 1.0),
                         (w_ret_br[0], 0, 0, D_MODEL, 1.0),
                         (w_ret_br[0], D_MODEL, 0, D_MODEL, 1.0)], D_MODEL)

    def layer(x, T, pos0, rows, init):
        B, seq, _ = x.shape
        h = _ffn_ln(x.reshape(B * seq, D_MODEL), wu1, wd1, ln1_g, ln1_b, rows)
        h, cbuf, s = _mixer(h.reshape(B, seq, D_MODEL), slab_a, slab_b, conv_w[0], ln2_g, ln2_b,
                            T, pos0, init)
        h = _ffn_ln(h.reshape(B * seq, D_MODEL), wu2, wd2, ln3_g, ln3_b, rows)
        return h.reshape(B, seq, D_MODEL), cbuf[None], s[None]

    yp, conv_p, ret_p = layer(x_prompt, MIXER_ROWS, 0, FFN_ROWS, None)
    ys, conv_s, ret_s = layer(x_sample, Ts, PAST_LEN, Bs * Ts, (cache_conv[0], state_ret[0]))
    return (yp, ys, conv_p, ret_p, conv_s, ret_s)
```

```python
import functools
import math

import jax
import jax.numpy as jnp
from jax import lax
from jax.experimental import pallas as pl
from jax.experimental.pallas import tpu as pltpu

D_MODEL = 1024
D_FF = 2816
D_CONV = 1024
CONV_W = 3
N_HEADS = 4
DK = 256
DV = 512
D_QK = N_HEADS * DK
D_V = N_HEADS * DV
PAST_LEN = 2048
ROPE_BASE = 10000.0
LN_EPS = 1e-5
GN_EPS = 1e-6
DEPTH = 1
ALPHA = (2 * DEPTH) ** 0.25
K_SCALE = DK ** -0.5
assert math.frexp(K_SCALE)[0] == 0.5, "the key scale is folded into the rotary tables"

OFF_CB = 0
OFF_CC = OFF_CB + D_CONV
OFF_CH = OFF_CC + D_CONV
OFF_Q = OFF_CH + D_CONV
OFF_K = OFF_Q + D_QK
OFF_V = OFF_K + D_QK
OFF_G = OFF_V + D_V
OFF_GC = OFF_G + D_V
OFF_GR = OFF_GC + D_MODEL
D_IN = OFF_GR + D_MODEL

LOG_GAMMA = tuple(math.log(1.0 - 2.0 ** (-5.0 - h)) for h in range(N_HEADS))

LANES = 128
W_SPLIT = D_IN // 2
SLAB_COLS = W_SPLIT + 2 * D_MODEL
assert (SLAB_COLS // LANES) % 8 != 0 and D_V == 2 * D_MODEL
OFF_WCB = D_IN
OFF_WOUT = OFF_WCB + D_MODEL
OFF_WRB = OFF_WOUT + D_MODEL
NORM_CHUNK = 64
PACK_STEPS = 4
FFN_ROWS = 1024
MIXER_ROWS = 256
VMEM_LIMIT_BYTES = 58 * 1024 * 1024

F32 = jnp.float32
BF16 = jnp.bfloat16


def _residual_norm(o_ref, start, n, x, h, g_ref, b_ref):
    o_ref[pl.ds(start, n), :] = x + h
    g, b = g_ref[...], b_ref[...]
    chunk = min(n, NORM_CHUNK)
    carry = None
    for r in range(start, start + n, chunk):
        z = o_ref[pl.ds(r, chunk), :]
        total = jnp.sum(z, axis=-1, keepdims=True)
        if carry is not None:
            total = total + carry
        mu = total * (1.0 / D_MODEL)
        d = z - mu
        var = jnp.mean(d * d, axis=-1, keepdims=True)
        out = d * lax.rsqrt(var + LN_EPS / ALPHA ** 2) * g + b
        o_ref[pl.ds(r, chunk), :] = out
        carry = jnp.minimum(jnp.maximum(out[:, :1], 0.0), 0.0)


def _dot(a, b):
    return jnp.dot(a, b, preferred_element_type=F32)


def _ffn_ln_kernel(x_ref, xnext_ref, wu_ref, wd_ref, g_ref, b_ref, o_ref, act_scr):
    half = x_ref.shape[0] // 2

    def gated_up(x):
        xb = x.astype(BF16)
        a = _dot(xb, wu_ref[:, :D_FF])
        b = _dot(xb, wu_ref[:, D_FF:])
        return (a * jax.nn.sigmoid(a) * b).astype(BF16)

    def down(act):
        return _dot(act, wd_ref[...])

    @pl.when(pl.program_id(0) == 0)
    def _():
        act_scr[...] = gated_up(x_ref[:half, :])

    x0 = x_ref[:half, :]
    x1 = x_ref[half:, :]
    y0 = down(act_scr[...])
    y1 = down(gated_up(x1))
    act_scr[...] = gated_up(xnext_ref[...])
    _residual_norm(o_ref, 0, half, x0, y0, g_ref, b_ref)
    _residual_norm(o_ref, half, half, x1, y1, g_ref, b_ref)


def _pack_kernel(*refs, scales):
    *piece_refs, o_ref = refs
    col = 0
    for p_ref, scale in zip(piece_refs, scales):
        width = p_ref.shape[-1]
        piece = p_ref[...] if scale == 1.0 else p_ref[...] * scale
        o_ref[:, col:col + width] = piece.astype(BF16)
        col += width


def _pack_bf16(pieces, rows):
    assert rows % (PACK_STEPS * 16) == 0
    step_rows = rows // PACK_STEPS
    in_specs, args = [], []
    for w, row0, col0, width, _ in pieces:
        assert row0 % step_rows == 0 and col0 % width == 0
        in_specs.append(pl.BlockSpec(
            (step_rows, width),
            functools.partial(lambda i, r, c: (i + r, c), r=row0 // step_rows, c=col0 // width)))
        args.append(w)
    total = sum(p[3] for p in pieces)
    return pl.pallas_call(
        functools.partial(_pack_kernel, scales=tuple(p[4] for p in pieces)),
        grid=(PACK_STEPS,),
        in_specs=in_specs,
        out_specs=pl.BlockSpec((step_rows, total), lambda i: (i, 0)),
        out_shape=jax.ShapeDtypeStruct((rows, total), BF16),
        compiler_params=pltpu.CompilerParams(
            dimension_semantics=("arbitrary",), vmem_limit_bytes=VMEM_LIMIT_BYTES),
        name="pack_bf16",
    )(*args)


def _ffn_ln_single_kernel(x_ref, wu_ref, wd_ref, g_ref, b_ref, o_ref):
    x = x_ref[...]
    xb = x.astype(BF16)
    a = _dot(xb, wu_ref[:, :D_FF])
    b = _dot(xb, wu_ref[:, D_FF:])
    y = _dot((a * jax.nn.sigmoid(a) * b).astype(BF16), wd_ref[...])
    _residual_norm(o_ref, 0, x_ref.shape[0], x, y, g_ref, b_ref)


def _const_spec(shape):
    zeros = (0,) * len(shape)
    return pl.BlockSpec(shape, lambda *_: zeros, pipeline_mode=pl.Buffered(1))


def _ffn_ln(x2d, w_up, w_down, g, b, rows):
    n = x2d.shape[0]
    assert n % rows == 0 and rows % 2 == 0
    n_tiles = n // rows
    half = rows // 2
    weight_specs = [
        _const_spec((D_MODEL, 2 * D_FF)),
        _const_spec((D_FF, D_MODEL)),
        _const_spec((1, D_MODEL)),
        _const_spec((1, D_MODEL)),
    ]
    params = pltpu.CompilerParams(
        dimension_semantics=("arbitrary",), vmem_limit_bytes=VMEM_LIMIT_BYTES)
    if n_tiles == 1:
        return pl.pallas_call(
            _ffn_ln_single_kernel,
            grid=(1,),
            in_specs=[pl.BlockSpec((rows, D_MODEL), lambda i: (0, 0))] + weight_specs,
            out_specs=pl.BlockSpec((rows, D_MODEL), lambda i: (0, 0)),
            out_shape=jax.ShapeDtypeStruct((n, D_MODEL), F32),
            compiler_params=params,
            name="ffn_ln_single",
        )(x2d, w_up, w_down, g, b)
    return pl.pallas_call(
        _ffn_ln_kernel,
        grid=(n_tiles,),
        in_specs=[
            pl.BlockSpec((rows, D_MODEL), lambda i: (i, 0)),
            pl.BlockSpec((half, D_MODEL), lambda i: (2 * jnp.minimum(i + 1, n_tiles - 1), 0)),
        ] + weight_specs,
        out_specs=pl.BlockSpec((rows, D_MODEL), lambda i: (i, 0)),
        out_shape=jax.ShapeDtypeStruct((n, D_MODEL), F32),
        scratch_shapes=[pltpu.VMEM((half, D_FF), BF16)],
        compiler_params=params,
        name="ffn_ln",
    )(x2d, x2d, w_up, w_down, g, b)


def _mixer_kernel(*refs, T, pos0, zero_init):
    if zero_init:
        (x_ref, xnext_ref, wa_ref, wb_ref, convw_ref, g_ref, b_ref,
         y_ref, cbuf_ref, s_ref, rope_scr, dec_scr, rowdec_scr, qk_scr) = refs
    else:
        (x_ref, xnext_ref, wa_ref, wb_ref, convw_ref, g_ref, b_ref,
         cbuf0_ref, s0_ref,
         y_ref, cbuf_ref, s_ref, rope_scr, dec_scr, rowdec_scr, qk_scr) = refs
    t = pl.program_id(1)

    @pl.when(t == 0)
    def _():
        if zero_init:
            cbuf_ref[...] = jnp.zeros_like(cbuf_ref)
            s_ref[...] = jnp.zeros_like(s_ref)
        else:
            cbuf_ref[...] = cbuf0_ref[...]
            s_ref[...] = s0_ref[...]

    half = DK // 2
    b = pl.program_id(0)

    @pl.when(b == 0)
    def _():
        frame = lax.broadcasted_iota(jnp.int32, (T, half), 0)
        pos = (frame + (pos0 + t * T)).astype(F32)
        lane = lax.broadcasted_iota(jnp.int32, (T, half), 1).astype(F32)
        ang = pos * jnp.exp(lane * (-math.log(ROPE_BASE) / half))
        rows = pl.ds(pl.multiple_of(t * T, T), T)
        rope_scr[0, rows, :] = jnp.cos(ang)
        rope_scr[1, rows, :] = jnp.sin(ang)
        rope_scr[2, rows, :] = jnp.cos(ang) * K_SCALE
        rope_scr[3, rows, :] = jnp.sin(ang) * K_SCALE

    @pl.when((b == 0) & (t == 0))
    def _():
        li = lax.broadcasted_iota(jnp.int32, (T, T), 0)
        mi = lax.broadcasted_iota(jnp.int32, (T, T), 1)
        diff = (li - mi).astype(F32)
        idx = lax.broadcasted_iota(jnp.int32, (T, LANES), 0).astype(F32)
        for h in range(N_HEADS):
            lg = LOG_GAMMA[h]
            dec_scr[h] = jnp.where(diff >= 0, jnp.exp(lg * jnp.maximum(diff, 0.0)), 0.0)
            rowdec_scr[0, h] = jnp.exp(lg * (idx + 1.0))
            rowdec_scr[1, h] = jnp.exp(lg * (T - 1.0 - idx))

    x = x_ref[0]
    xb = x.astype(BF16)

    rows = pl.ds(pl.multiple_of(t * T, T), T)

    def weight(lo, width):
        if lo + width <= W_SPLIT:
            return wa_ref[:, lo:lo + width]
        if lo >= OFF_WRB:
            return wb_ref[:, lo - OFF_WRB + D_IN - W_SPLIT:lo - OFF_WRB + D_IN - W_SPLIT + width]
        if lo >= OFF_WCB:
            return wa_ref[:, lo - OFF_WCB + W_SPLIT:lo - OFF_WCB + W_SPLIT + width]
        assert lo >= W_SPLIT
        return wb_ref[:, lo - W_SPLIT:lo - W_SPLIT + width]

    def proj(lo, width, lhs=xb):
        return _dot(lhs, weight(lo, width))

    @pl.when((b == 0) & (t == 0))
    def _():
        qk_scr[...] = proj(OFF_Q, 2 * D_QK)

    def rotary(a, plane):
        cos, sin = rope_scr[plane, rows, :], rope_scr[plane + 1, rows, :]
        a1, a2 = a[:, :half], a[:, half:]
        return jnp.concatenate([a1 * cos - a2 * sin, a1 * sin + a2 * cos], axis=-1)

    def lane_tiled(a, width):
        return jnp.concatenate([a] * (width // LANES), axis=-1)

    def conv_input():
        u = proj(OFF_CC, D_CONV) * proj(OFF_CH, D_CONV)
        prev = cbuf_ref[0]
        row = lax.broadcasted_iota(jnp.int32, (T, D_CONV), 0)
        u1 = jnp.where(row == 0, prev[1:2], pltpu.roll(u, 1, 0))
        u2 = jnp.where(row == 0, prev[0:1], jnp.where(row == 1, prev[1:2], pltpu.roll(u, 2, 0)))
        cw = convw_ref[...]
        cbuf_ref[0] = u[T - 2:T]
        return u2 * cw[0:1] + u1 * cw[1:2] + u * cw[2:3]

    def retention_scores(h, q, k):
        qh = rotary(q[:, h * DK:(h + 1) * DK], 0)
        kh = rotary(k[:, h * DK:(h + 1) * DK], 2)
        qb = qh.astype(BF16)
        scores = lax.dot_general(qb, kh.astype(BF16), (((1,), (1,)), ((), ())),
                                 preferred_element_type=F32) * dec_scr[h]
        return qh, kh, qb, scores.astype(BF16)

    def retention_output(h, qh, kh, qb, scores, v):
        lg = LOG_GAMMA[h]
        vb = v[h].astype(BF16)
        s_old = s_ref[0, h]
        if T % LANES == 0:
            qc = (qh * lane_tiled(rowdec_scr[0, h], DK)).astype(BF16)
            o = _dot(jnp.concatenate([scores, qc], axis=-1),
                     jnp.concatenate([vb, s_old.astype(BF16)], axis=0))
        else:
            intra = _dot(scores, vb)
            cross = _dot(qb, s_old.astype(BF16)) * lane_tiled(rowdec_scr[0, h], DV)
            o = intra + cross
        kdec = (kh * lane_tiled(rowdec_scr[1, h], DK)).astype(BF16)
        s_ref[0, h] = math.exp(lg * T) * s_old + lax.dot_general(
            kdec, vb, (((0,), (0,)), ((), ())), preferred_element_type=F32)
        return o

    def normed_gated(o, gh, chained):
        chunk = min(T, NORM_CHUNK) if chained else T
        pieces, carry = [], None
        for r in range(0, T, chunk):
            oc, gc = o[r:r + chunk], gh[r:r + chunk]
            total = jnp.sum(oc, axis=-1, keepdims=True)
            if carry is not None:
                total = total + carry
            d = oc - total * (1.0 / DV)
            var = jnp.mean(d * d, axis=-1, keepdims=True)
            out = d * lax.rsqrt(var + GN_EPS) * (gc * jax.nn.sigmoid(gc))
            pieces.append(out.astype(BF16))
            carry = jnp.minimum(jnp.maximum(out[:, :1], 0.0), 0.0)
        return jnp.concatenate(pieces, axis=0)

    q = qk_scr[:, :D_QK]
    k = qk_scr[:, D_QK:]
    v = [proj(OFF_V + h * DV, DV) for h in range(N_HEADS)]
    gated = []
    yconv = z = m_conv = gate_r = None
    for h in range(N_HEADS):
        head = retention_scores(h, q, k)
        gh = proj(OFF_G + h * DV, DV)
        o = retention_output(h, *head, v)
        if h == 0:
            yconv = conv_input()
        elif h == 1:
            z = (proj(OFF_CB, D_CONV) * yconv).astype(BF16)
        elif h == 2:
            m_conv = jax.nn.sigmoid(proj(OFF_GC, D_MODEL)) * _dot(z, weight(OFF_WCB, D_MODEL))
        else:
            gate_r = jax.nn.sigmoid(proj(OFF_GR, D_MODEL))
        gated.append(normed_gated(o, gh, chained=h < N_HEADS - 1))
    y_ret = (_dot(jnp.concatenate(gated[:2], axis=-1), weight(OFF_WRB, D_MODEL))
             + _dot(jnp.concatenate(gated[2:], axis=-1), weight(OFF_WRB + D_MODEL, D_MODEL)))

    m = m_conv + gate_r * y_ret
    hmix = _dot(m.astype(BF16), weight(OFF_WOUT, D_MODEL))
    qk_scr[...] = proj(OFF_Q, 2 * D_QK, xnext_ref[0].astype(BF16))
    _residual_norm(y_ref.at[0], 0, T, x, hmix, g_ref, b_ref)


def _mixer(x, slab_a, slab_b, conv_w, g, b, T, pos0, init=None):
    B, seq, _ = x.shape
    assert seq % T == 0
    zero_init = init is None
    n_t = seq // T

    def next_step(i, t):
        lin = jnp.minimum(i * n_t + t + 1, B * n_t - 1)
        return (lin // n_t, lin % n_t, 0)

    in_specs = [
        pl.BlockSpec((1, T, D_MODEL), lambda i, t: (i, t, 0)),
        pl.BlockSpec((1, T, D_MODEL), next_step),
        _const_spec((D_MODEL, SLAB_COLS)),
        _const_spec((D_MODEL, SLAB_COLS)),
        _const_spec((CONV_W, D_CONV)),
        _const_spec((1, D_MODEL)),
        _const_spec((1, D_MODEL)),
    ]
    args = [x, x, slab_a, slab_b, conv_w, g, b]
    cbuf_spec = pl.BlockSpec((1, CONV_W - 1, D_CONV), lambda i, t: (i, 0, 0))
    s_spec = pl.BlockSpec((1, N_HEADS, DK, DV), lambda i, t: (i, 0, 0, 0))
    if not zero_init:
        in_specs += [cbuf_spec, s_spec]
        args += list(init)
    return pl.pallas_call(
        functools.partial(_mixer_kernel, T=T, pos0=pos0, zero_init=zero_init),
        grid=(B, seq // T),
        in_specs=in_specs,
        out_specs=[pl.BlockSpec((1, T, D_MODEL), lambda i, t: (i, t, 0)), cbuf_spec, s_spec],
        out_shape=[
            jax.ShapeDtypeStruct((B, seq, D_MODEL), F32),
            jax.ShapeDtypeStruct((B, CONV_W - 1, D_CONV), F32),
            jax.ShapeDtypeStruct((B, N_HEADS, DK, DV), F32),
        ],
        scratch_shapes=[
            pltpu.VMEM((4, seq, DK // 2), F32),
            pltpu.VMEM((N_HEADS, T, T), F32),
            pltpu.VMEM((2, N_HEADS, T, LANES), F32),
            pltpu.VMEM((T, 2 * D_QK), F32),
        ],
        compiler_params=pltpu.CompilerParams(
            dimension_semantics=("arbitrary", "arbitrary"), vmem_limit_bytes=VMEM_LIMIT_BYTES),
        name="mixer_prompt" if zero_init else "mixer_sample",
    )(*args)


def kernel(x_prompt, x_sample, cache_conv, state_ret, ln1_g, ln1_b, ffn1_w_up, ffn1_w_down, w_in,
           conv_w, w_conv_br, w_ret_br, w_out, ln2_g, ln2_b, ffn2_w_up, ffn2_w_down, ln3_g, ln3_b):
    assert w_in.shape[0] == DEPTH == 1
    Bp, Tp, _ = x_prompt.shape
    Bs, Ts, _ = x_sample.shape

    def cast(w, scale=1.0):
        rows, cols = w.shape[1:]
        return _pack_bf16([(w[0], 0, 0, cols, scale)], rows)

    wu1, wd1 = cast(ffn1_w_up), cast(ffn1_w_down, 0.5 / ALPHA)
    wu2, wd2 = cast(ffn2_w_up), cast(ffn2_w_down, 0.5 / ALPHA)
    slab_a = _pack_bf16([(w_in[0], 0, 0, W_SPLIT, 1.0), (w_conv_br[0], 0, 0, D_MODEL, 1.0),
                         (w_out[0], 0, 0, D_MODEL, 1.0 / ALPHA)], D_MODEL)
    slab_b = _pack_bf16([(w_in[0], 0, W_SPLIT, D_IN - W_SPLIT, 1.0),
                         (w_ret_br[0], 0, 0, D_MODEL, 1.0),
                         (w_ret_br[0], D_MODEL, 0, D_MODEL, 1.0)], D_MODEL)

    def layer(x, T, pos0, rows, init):
        B, seq, _ = x.shape
        h = _ffn_ln(x.reshape(B * seq, D_MODEL), wu1, wd1, ln1_g, ln1_b, rows)
        h, cbuf, s = _mixer(h.reshape(B, seq, D_MODEL), slab_a, slab_b, conv_w[0], ln2_g, ln2_b,
                            T, pos0, init)
        h = _ffn_ln(h.reshape(B * seq, D_MODEL), wu2, wd2, ln3_g, ln3_b, rows)
        return h.reshape(B, seq, D_MODEL), cbuf[None], s[None]

    yp, conv_p, ret_p = layer(x_prompt, MIXER_ROWS, 0, FFN_ROWS, None)
    ys, conv_s, ret_s = layer(x_sample, Ts, PAST_LEN, Bs * Ts, (cache_conv[0], state_ret[0]))
    return (yp, ys, conv_p, ret_p, conv_s, ret_s)
```

```python
import functools
import math

import jax
import jax.numpy as jnp
from jax import lax
from jax.experimental import pallas as pl
from jax.experimental.pallas import tpu as pltpu

D_MODEL = 1024
D_FF = 2816
D_CONV = 1024
CONV_W = 3
N_HEADS = 4
DK = 256
DV = 512
D_QK = N_HEADS * DK
D_V = N_HEADS * DV
PAST_LEN = 2048
ROPE_BASE = 10000.0
LN_EPS = 1e-5
GN_EPS = 1e-6
DEPTH = 1
ALPHA = (2 * DEPTH) ** 0.25
K_SCALE = DK ** -0.5
assert math.frexp(K_SCALE)[0] == 0.5, "the key scale is folded into the rotary tables"

OFF_CB = 0
OFF_CC = OFF_CB + D_CONV
OFF_CH = OFF_CC + D_CONV
OFF_Q = OFF_CH + D_CONV
OFF_K = OFF_Q + D_QK
OFF_V = OFF_K + D_QK
OFF_G = OFF_V + D_V
OFF_GC = OFF_G + D_V
OFF_GR = OFF_GC + D_MODEL
D_IN = OFF_GR + D_MODEL

LOG_GAMMA = tuple(math.log(1.0 - 2.0 ** (-5.0 - h)) for h in range(N_HEADS))

LANES = 128
W_SPLIT = D_IN // 2
SLAB_COLS = W_SPLIT + 2 * D_MODEL
assert (SLAB_COLS // LANES) % 8 != 0 and D_V == 2 * D_MODEL
OFF_WCB = D_IN
OFF_WOUT = OFF_WCB + D_MODEL
OFF_WRB = OFF_WOUT + D_MODEL
NORM_CHUNK = 64
PACK_STEPS = 4
FFN_ROWS = 1024
MIXER_ROWS = 256
MIXER_TILES = 2
VMEM_LIMIT_BYTES = 61 * 1024 * 1024

F32 = jnp.float32
BF16 = jnp.bfloat16


def _residual_norm(o_ref, start, n, x, h, g_ref, b_ref):
    o_ref[pl.ds(start, n), :] = x + h
    g, b = g_ref[...], b_ref[...]
    chunk = min(n, NORM_CHUNK)
    carry = None
    for r in range(start, start + n, chunk):
        z = o_ref[pl.ds(r, chunk), :]
        total = jnp.sum(z, axis=-1, keepdims=True)
        if carry is not None:
            total = total + carry
        mu = total * (1.0 / D_MODEL)
        d = z - mu
        var = jnp.mean(d * d, axis=-1, keepdims=True)
        out = d * lax.rsqrt(var + LN_EPS / ALPHA ** 2) * g + b
        o_ref[pl.ds(r, chunk), :] = out
        carry = jnp.minimum(jnp.maximum(out[:, :1], 0.0), 0.0)


def _dot(a, b):
    return jnp.dot(a, b, preferred_element_type=F32)


def _ffn_ln_kernel(x_ref, xnext_ref, wu_ref, wd_ref, g_ref, b_ref, o_ref, act_scr):
    half = x_ref.shape[0] // 2

    def gated_up(x):
        xb = x.astype(BF16)
        a = _dot(xb, wu_ref[:, :D_FF])
        b = _dot(xb, wu_ref[:, D_FF:])
        return (a * jax.nn.sigmoid(a) * b).astype(BF16)

    def down(act):
        return _dot(act, wd_ref[...])

    @pl.when(pl.program_id(0) == 0)
    def _():
        act_scr[...] = gated_up(x_ref[:half, :])

    x0 = x_ref[:half, :]
    x1 = x_ref[half:, :]
    y0 = down(act_scr[...])
    y1 = down(gated_up(x1))
    act_scr[...] = gated_up(xnext_ref[...])
    _residual_norm(o_ref, 0, half, x0, y0, g_ref, b_ref)
    _residual_norm(o_ref, half, half, x1, y1, g_ref, b_ref)


def _pack_kernel(*refs, scales):
    *piece_refs, o_ref = refs
    col = 0
    for p_ref, scale in zip(piece_refs, scales):
        width = p_ref.shape[-1]
        piece = p_ref[...] if scale == 1.0 else p_ref[...] * scale
        o_ref[:, col:col + width] = piece.astype(BF16)
        col += width


def _pack_bf16(pieces, rows):
    assert rows % (PACK_STEPS * 16) == 0
    step_rows = rows // PACK_STEPS
    in_specs, args = [], []
    for w, row0, col0, width, _ in pieces:
        assert row0 % step_rows == 0 and col0 % width == 0
        in_specs.append(pl.BlockSpec(
            (step_rows, width),
            functools.partial(lambda i, r, c: (i + r, c), r=row0 // step_rows, c=col0 // width)))
        args.append(w)
    total = sum(p[3] for p in pieces)
    return pl.pallas_call(
        functools.partial(_pack_kernel, scales=tuple(p[4] for p in pieces)),
        grid=(PACK_STEPS,),
        in_specs=in_specs,
        out_specs=pl.BlockSpec((step_rows, total), lambda i: (i, 0)),
        out_shape=jax.ShapeDtypeStruct((rows, total), BF16),
        compiler_params=pltpu.CompilerParams(
            dimension_semantics=("arbitrary",), vmem_limit_bytes=VMEM_LIMIT_BYTES),
        name="pack_bf16",
    )(*args)


def _ffn_ln_single_kernel(x_ref, wu_ref, wd_ref, g_ref, b_ref, o_ref):
    x = x_ref[...]
    xb = x.astype(BF16)
    a = _dot(xb, wu_ref[:, :D_FF])
    b = _dot(xb, wu_ref[:, D_FF:])
    y = _dot((a * jax.nn.sigmoid(a) * b).astype(BF16), wd_ref[...])
    _residual_norm(o_ref, 0, x_ref.shape[0], x, y, g_ref, b_ref)


def _const_spec(shape):
    zeros = (0,) * len(shape)
    return pl.BlockSpec(shape, lambda *_: zeros, pipeline_mode=pl.Buffered(1))


def _ffn_ln(x2d, w_up, w_down, g, b, rows):
    n = x2d.shape[0]
    assert n % rows == 0 and rows % 2 == 0
    n_tiles = n // rows
    half = rows // 2
    weight_specs = [
        _const_spec((D_MODEL, 2 * D_FF)),
        _const_spec((D_FF, D_MODEL)),
        _const_spec((1, D_MODEL)),
        _const_spec((1, D_MODEL)),
    ]
    params = pltpu.CompilerParams(
        dimension_semantics=("arbitrary",), vmem_limit_bytes=VMEM_LIMIT_BYTES)
    if n_tiles == 1:
        return pl.pallas_call(
            _ffn_ln_single_kernel,
            grid=(1,),
            in_specs=[pl.BlockSpec((rows, D_MODEL), lambda i: (0, 0))] + weight_specs,
            out_specs=pl.BlockSpec((rows, D_MODEL), lambda i: (0, 0)),
            out_shape=jax.ShapeDtypeStruct((n, D_MODEL), F32),
            compiler_params=params,
            name="ffn_ln_single",
        )(x2d, w_up, w_down, g, b)
    return pl.pallas_call(
        _ffn_ln_kernel,
        grid=(n_tiles,),
        in_specs=[
            pl.BlockSpec((rows, D_MODEL), lambda i: (i, 0)),
            pl.BlockSpec((half, D_MODEL), lambda i: (2 * jnp.minimum(i + 1, n_tiles - 1), 0)),
        ] + weight_specs,
        out_specs=pl.BlockSpec((rows, D_MODEL), lambda i: (i, 0)),
        out_shape=jax.ShapeDtypeStruct((n, D_MODEL), F32),
        scratch_shapes=[pltpu.VMEM((half, D_FF), BF16)],
        compiler_params=params,
        name="ffn_ln",
    )(x2d, x2d, w_up, w_down, g, b)


def _mixer_kernel(*refs, T, tiles, pos0, zero_init):
    if zero_init:
        (x_ref, xnext_ref, wa_ref, wb_ref, convw_ref, g_ref, b_ref,
         y_ref, cbuf_ref, s_ref, rope_scr, dec_scr, rowdec_scr, qk_scr) = refs
    else:
        (x_ref, xnext_ref, wa_ref, wb_ref, convw_ref, g_ref, b_ref,
         cbuf0_ref, s0_ref,
         y_ref, cbuf_ref, s_ref, rope_scr, dec_scr, rowdec_scr, qk_scr) = refs
    t = pl.program_id(1)

    @pl.when(t == 0)
    def _():
        if zero_init:
            cbuf_ref[...] = jnp.zeros_like(cbuf_ref)
            s_ref[...] = jnp.zeros_like(s_ref)
        else:
            cbuf_ref[...] = cbuf0_ref[...]
            s_ref[...] = s0_ref[...]

    half = DK // 2
    b = pl.program_id(0)

    @pl.when(b == 0)
    def _():
        step_rows = tiles * T
        frame = lax.broadcasted_iota(jnp.int32, (step_rows, half), 0)
        pos = (frame + (pos0 + t * step_rows)).astype(F32)
        lane = lax.broadcasted_iota(jnp.int32, (step_rows, half), 1).astype(F32)
        ang = pos * jnp.exp(lane * (-math.log(ROPE_BASE) / half))
        rows = pl.ds(pl.multiple_of(t * step_rows, step_rows), step_rows)
        rope_scr[0, rows, :] = jnp.cos(ang)
        rope_scr[1, rows, :] = jnp.sin(ang)
        rope_scr[2, rows, :] = jnp.cos(ang) * K_SCALE
        rope_scr[3, rows, :] = jnp.sin(ang) * K_SCALE

    @pl.when((b == 0) & (t == 0))
    def _():
        li = lax.broadcasted_iota(jnp.int32, (T, T), 0)
        mi = lax.broadcasted_iota(jnp.int32, (T, T), 1)
        diff = (li - mi).astype(F32)
        idx = lax.broadcasted_iota(jnp.int32, (T, LANES), 0).astype(F32)
        for h in range(N_HEADS):
            lg = LOG_GAMMA[h]
            dec_scr[h] = jnp.where(diff >= 0, jnp.exp(lg * jnp.maximum(diff, 0.0)), 0.0)
            rowdec_scr[0, h] = jnp.exp(lg * (idx + 1.0))
            rowdec_scr[1, h] = jnp.exp(lg * (T - 1.0 - idx))

    def weight(lo, width):
        if lo + width <= W_SPLIT:
            return wa_ref[:, lo:lo + width]
        if lo >= OFF_WRB:
            return wb_ref[:, lo - OFF_WRB + D_IN - W_SPLIT:lo - OFF_WRB + D_IN - W_SPLIT + width]
        if lo >= OFF_WCB:
            return wa_ref[:, lo - OFF_WCB + W_SPLIT:lo - OFF_WCB + W_SPLIT + width]
        assert lo >= W_SPLIT
        return wb_ref[:, lo - W_SPLIT:lo - W_SPLIT + width]

    def tile(j):
        x = x_ref[0, j * T:(j + 1) * T, :]
        xb = x.astype(BF16)
        x_following = x_ref[0, (j + 1) * T:(j + 2) * T, :] if j + 1 < tiles else xnext_ref[0]
        rows = pl.ds(pl.multiple_of((t * tiles + j) * T, T), T)

        def proj(lo, width, lhs=xb):
            return _dot(lhs, weight(lo, width))

        if j == 0:
            @pl.when((b == 0) & (t == 0))
            def _():
                qk_scr[...] = proj(OFF_Q, 2 * D_QK)

        def rotary(a, plane):
            cos, sin = rope_scr[plane, rows, :], rope_scr[plane + 1, rows, :]
            a1, a2 = a[:, :half], a[:, half:]
            return jnp.concatenate([a1 * cos - a2 * sin, a1 * sin + a2 * cos], axis=-1)

        def lane_tiled(a, width):
            return jnp.concatenate([a] * (width // LANES), axis=-1)

        def conv_input():
            u = proj(OFF_CC, D_CONV) * proj(OFF_CH, D_CONV)
            prev = cbuf_ref[0]
            row = lax.broadcasted_iota(jnp.int32, (T, D_CONV), 0)
            u1 = jnp.where(row == 0, prev[1:2], pltpu.roll(u, 1, 0))
            u2 = jnp.where(row == 0, prev[0:1], jnp.where(row == 1, prev[1:2], pltpu.roll(u, 2, 0)))
            cw = convw_ref[...]
            cbuf_ref[0] = u[T - 2:T]
            return u2 * cw[0:1] + u1 * cw[1:2] + u * cw[2:3]

        def retention_scores(h, q, k):
            qh = rotary(q[:, h * DK:(h + 1) * DK], 0)
            kh = rotary(k[:, h * DK:(h + 1) * DK], 2)
            qb = qh.astype(BF16)
            scores = lax.dot_general(qb, kh.astype(BF16), (((1,), (1,)), ((), ())),
                                     preferred_element_type=F32) * dec_scr[h]
            return qh, kh, qb, scores.astype(BF16)

        def retention_output(h, qh, kh, qb, scores, v):
            lg = LOG_GAMMA[h]
            vb = v[h].astype(BF16)
            s_old = s_ref[0, h]
            if T % LANES == 0:
                qc = (qh * lane_tiled(rowdec_scr[0, h], DK)).astype(BF16)
                o = _dot(jnp.concatenate([scores, qc], axis=-1),
                         jnp.concatenate([vb, s_old.astype(BF16)], axis=0))
            else:
                intra = _dot(scores, vb)
                cross = _dot(qb, s_old.astype(BF16)) * lane_tiled(rowdec_scr[0, h], DV)
                o = intra + cross
            kdec = (kh * lane_tiled(rowdec_scr[1, h], DK)).astype(BF16)
            s_ref[0, h] = math.exp(lg * T) * s_old + lax.dot_general(
                kdec, vb, (((0,), (0,)), ((), ())), preferred_element_type=F32)
            return o

        def normed_gated(o, gh, chained):
            chunk = min(T, NORM_CHUNK) if chained else T
            pieces, carry = [], None
            for r in range(0, T, chunk):
                oc, gc = o[r:r + chunk], gh[r:r + chunk]
                total = jnp.sum(oc, axis=-1, keepdims=True)
                if carry is not None:
                    total = total + carry
                d = oc - total * (1.0 / DV)
                var = jnp.mean(d * d, axis=-1, keepdims=True)
                out = d * lax.rsqrt(var + GN_EPS) * (gc * jax.nn.sigmoid(gc))
                pieces.append(out.astype(BF16))
                carry = jnp.minimum(jnp.maximum(out[:, :1], 0.0), 0.0)
            return jnp.concatenate(pieces, axis=0)

        q = qk_scr[:, :D_QK]
        k = qk_scr[:, D_QK:]
        v = [proj(OFF_V + h * DV, DV) for h in range(N_HEADS)]
        gated = []
        yconv = z = m_conv = gate_r = None
        for h in range(N_HEADS):
            head = retention_scores(h, q, k)
            gh = proj(OFF_G + h * DV, DV)
            o = retention_output(h, *head, v)
            if h == 0:
                yconv = conv_input()
            elif h == 1:
                z = (proj(OFF_CB, D_CONV) * yconv).astype(BF16)
            elif h == 2:
                m_conv = jax.nn.sigmoid(proj(OFF_GC, D_MODEL)) * _dot(z, weight(OFF_WCB, D_MODEL))
            else:
                gate_r = jax.nn.sigmoid(proj(OFF_GR, D_MODEL))
            gated.append(normed_gated(o, gh, chained=h < N_HEADS - 1))
        y_ret = (_dot(jnp.concatenate(gated[:2], axis=-1), weight(OFF_WRB, D_MODEL))
                 + _dot(jnp.concatenate(gated[2:], axis=-1), weight(OFF_WRB + D_MODEL, D_MODEL)))

        m = m_conv + gate_r * y_ret
        hmix = _dot(m.astype(BF16), weight(OFF_WOUT, D_MODEL))
        qk_scr[...] = proj(OFF_Q, 2 * D_QK, x_following.astype(BF16))
        _residual_norm(y_ref.at[0], j * T, T, x, hmix, g_ref, b_ref)

    for j in range(tiles):
        tile(j)


def _mixer(x, slab_a, slab_b, conv_w, g, b, T, tiles, pos0, init=None):
    B, seq, _ = x.shape
    assert seq % (T * tiles) == 0
    zero_init = init is None
    n_t = seq // (T * tiles)

    def next_step(i, t):
        lin = jnp.minimum(i * n_t + t + 1, B * n_t - 1)
        return (lin // n_t, (lin % n_t) * tiles, 0)

    in_specs = [
        pl.BlockSpec((1, tiles * T, D_MODEL), lambda i, t: (i, t, 0)),
        pl.BlockSpec((1, T, D_MODEL), next_step),
        _const_spec((D_MODEL, SLAB_COLS)),
        _const_spec((D_MODEL, SLAB_COLS)),
        _const_spec((CONV_W, D_CONV)),
        _const_spec((1, D_MODEL)),
        _const_spec((1, D_MODEL)),
    ]
    args = [x, x, slab_a, slab_b, conv_w, g, b]
    cbuf_spec = pl.BlockSpec((1, CONV_W - 1, D_CONV), lambda i, t: (i, 0, 0))
    s_spec = pl.BlockSpec((1, N_HEADS, DK, DV), lambda i, t: (i, 0, 0, 0))
    if not zero_init:
        in_specs += [cbuf_spec, s_spec]
        args += list(init)
    return pl.pallas_call(
        functools.partial(_mixer_kernel, T=T, tiles=tiles, pos0=pos0, zero_init=zero_init),
        grid=(B, n_t),
        in_specs=in_specs,
        out_specs=[pl.BlockSpec((1, tiles * T, D_MODEL), lambda i, t: (i, t, 0)), cbuf_spec, s_spec],
        out_shape=[
            jax.ShapeDtypeStruct((B, seq, D_MODEL), F32),
            jax.ShapeDtypeStruct((B, CONV_W - 1, D_CONV), F32),
            jax.ShapeDtypeStruct((B, N_HEADS, DK, DV), F32),
        ],
        scratch_shapes=[
            pltpu.VMEM((4, seq, DK // 2), F32),
            pltpu.VMEM((N_HEADS, T, T), F32),
            pltpu.VMEM((2, N_HEADS, T, LANES), F32),
            pltpu.VMEM((T, 2 * D_QK), F32),
        ],
        compiler_params=pltpu.CompilerParams(
            dimension_semantics=("arbitrary", "arbitrary"), vmem_limit_bytes=VMEM_LIMIT_BYTES),
        name="mixer_prompt" if zero_init else "mixer_sample",
    )(*args)


def kernel(x_prompt, x_sample, cache_conv, state_ret, ln1_g, ln1_b, ffn1_w_up, ffn1_w_down, w_in,
           conv_w, w_conv_br, w_ret_br, w_out, ln2_g, ln2_b, ffn2_w_up, ffn2_w_down, ln3_g, ln3_b):
    assert w_in.shape[0] == DEPTH == 1
    Bp, Tp, _ = x_prompt.shape
    Bs, Ts, _ = x_sample.shape

    def cast(w, scale=1.0):
        rows, cols = w.shape[1:]
        return _pack_bf16([(w[0], 0, 0, cols, scale)], rows)

    wu1, wd1 = cast(ffn1_w_up), cast(ffn1_w_down, 0.5 / ALPHA)
    wu2, wd2 = cast(ffn2_w_up), cast(ffn2_w_down, 0.5 / ALPHA)
    slab_a = _pack_bf16([(w_in[0], 0, 0, W_SPLIT, 1.0), (w_conv_br[0], 0, 0, D_MODEL, 1.0),
                         (w_out[0], 0, 0, D_MODEL, 1.0 / ALPHA)], D_MODEL)
    slab_b = _pack_bf16([(w_in[0], 0, W_SPLIT, D_IN - W_SPLIT, 1.0),
                         (w_ret_br[0], 0, 0, D_MODEL, 1.0),
                         (w_ret_br[0], D_MODEL, 0, D_MODEL, 1.0)], D_MODEL)

    def layer(x, T, tiles, pos0, rows, init):
        B, seq, _ = x.shape
        h = _ffn_ln(x.reshape(B * seq, D_MODEL), wu1, wd1, ln1_g, ln1_b, rows)
        h, cbuf, s = _mixer(h.reshape(B, seq, D_MODEL), slab_a, slab_b, conv_w[0], ln2_g, ln2_b,
                            T, tiles, pos0, init)
        h = _ffn_ln(h.reshape(B * seq, D_MODEL), wu2, wd2, ln3_g, ln3_b, rows)
        return h.reshape(B, seq, D_MODEL), cbuf[None], s[None]

    yp, conv_p, ret_p = layer(x_prompt, MIXER_ROWS, MIXER_TILES, 0, FFN_ROWS, None)
    ys, conv_s, ret_s = layer(x_sample, Ts, 1, PAST_LEN, Bs * Ts, (cache_conv[0], state_ret[0]))
    return (yp, ys, conv_p, ret_p, conv_s, ret_s)
```

```python
import functools
import math

import jax
import jax.numpy as jnp
from jax import lax
from jax.experimental import pallas as pl
from jax.experimental.pallas import tpu as pltpu

D_MODEL = 1024
D_FF = 2816
D_CONV = 1024
CONV_W = 3
N_HEADS = 4
DK = 256
DV = 512
D_QK = N_HEADS * DK
D_V = N_HEADS * DV
PAST_LEN = 2048
ROPE_BASE = 10000.0
LN_EPS = 1e-5
GN_EPS = 1e-6
DEPTH = 1
ALPHA = (2 * DEPTH) ** 0.25
K_SCALE = DK ** -0.5
assert math.frexp(K_SCALE)[0] == 0.5, "the key scale is folded into the rotary tables"

OFF_CB = 0
OFF_CC = OFF_CB + D_CONV
OFF_CH = OFF_CC + D_CONV
OFF_Q = OFF_CH + D_CONV
OFF_K = OFF_Q + D_QK
OFF_V = OFF_K + D_QK
OFF_G = OFF_V + D_V
OFF_GC = OFF_G + D_V
OFF_GR = OFF_GC + D_MODEL
D_IN = OFF_GR + D_MODEL

LOG_GAMMA = tuple(math.log(1.0 - 2.0 ** (-5.0 - h)) for h in range(N_HEADS))

LANES = 128
W_SPLIT = D_IN // 2
SLAB_COLS = W_SPLIT + 2 * D_MODEL
assert (SLAB_COLS // LANES) % 8 != 0 and D_V == 2 * D_MODEL
OFF_WCB = D_IN
OFF_WOUT = OFF_WCB + D_MODEL
OFF_WRB = OFF_WOUT + D_MODEL
NORM_CHUNK = 64
PACK_STEPS = 4
FFN_ROWS = 1024
MIXER_ROWS = 256
VMEM_LIMIT_BYTES = 58 * 1024 * 1024

F32 = jnp.float32
BF16 = jnp.bfloat16


def _residual_norm(o_ref, start, n, x, h, g_ref, b_ref):
    o_ref[pl.ds(start, n), :] = x + h
    g, b = g_ref[...], b_ref[...]
    chunk = min(n, NORM_CHUNK)
    carry = None
    for r in range(start, start + n, chunk):
        z = o_ref[pl.ds(r, chunk), :]
        total = jnp.sum(z, axis=-1, keepdims=True)
        if carry is not None:
            total = total + carry
        mu = total * (1.0 / D_MODEL)
        d = z - mu
        var = jnp.mean(d * d, axis=-1, keepdims=True)
        out = d * lax.rsqrt(var + LN_EPS / ALPHA ** 2) * g + b
        o_ref[pl.ds(r, chunk), :] = out
        carry = jnp.minimum(jnp.maximum(out[:, :1], 0.0), 0.0)


def _dot(a, b):
    return jnp.dot(a, b, preferred_element_type=F32)


def _swish_times(a, b):
    t = 0.5 * a
    return (jnp.tanh(t) + 1.0) * t * b


def _ffn_ln_kernel(x_ref, xnext_ref, wu_ref, wd_ref, g_ref, b_ref, o_ref, act_scr):
    half = x_ref.shape[0] // 2

    def gated_up(x):
        xb = x.astype(BF16)
        a = _dot(xb, wu_ref[:, :D_FF])
        b = _dot(xb, wu_ref[:, D_FF:])
        return _swish_times(a, b).astype(BF16)

    def down(act):
        return _dot(act, wd_ref[...])

    @pl.when(pl.program_id(0) == 0)
    def _():
        act_scr[...] = gated_up(x_ref[:half, :])

    x0 = x_ref[:half, :]
    x1 = x_ref[half:, :]
    y0 = down(act_scr[...])
    y1 = down(gated_up(x1))
    act_scr[...] = gated_up(xnext_ref[...])
    _residual_norm(o_ref, 0, half, x0, y0, g_ref, b_ref)
    _residual_norm(o_ref, half, half, x1, y1, g_ref, b_ref)


def _pack_kernel(*refs, scales):
    *piece_refs, o_ref = refs
    col = 0
    for p_ref, scale in zip(piece_refs, scales):
        width = p_ref.shape[-1]
        piece = p_ref[...] if scale == 1.0 else p_ref[...] * scale
        o_ref[:, col:col + width] = piece.astype(BF16)
        col += width


def _pack_bf16(pieces, rows):
    assert rows % (PACK_STEPS * 16) == 0
    step_rows = rows // PACK_STEPS
    in_specs, args = [], []
    for w, row0, col0, width, _ in pieces:
        assert row0 % step_rows == 0 and col0 % width == 0
        in_specs.append(pl.BlockSpec(
            (step_rows, width),
            functools.partial(lambda i, r, c: (i + r, c), r=row0 // step_rows, c=col0 // width)))
        args.append(w)
    total = sum(p[3] for p in pieces)
    return pl.pallas_call(
        functools.partial(_pack_kernel, scales=tuple(p[4] for p in pieces)),
        grid=(PACK_STEPS,),
        in_specs=in_specs,
        out_specs=pl.BlockSpec((step_rows, total), lambda i: (i, 0)),
        out_shape=jax.ShapeDtypeStruct((rows, total), BF16),
        compiler_params=pltpu.CompilerParams(
            dimension_semantics=("arbitrary",), vmem_limit_bytes=VMEM_LIMIT_BYTES),
        name="pack_bf16",
    )(*args)


def _ffn_ln_single_kernel(x_ref, wu_ref, wd_ref, g_ref, b_ref, o_ref):
    x = x_ref[...]
    xb = x.astype(BF16)
    a = _dot(xb, wu_ref[:, :D_FF])
    b = _dot(xb, wu_ref[:, D_FF:])
    y = _dot(_swish_times(a, b).astype(BF16), wd_ref[...])
    _residual_norm(o_ref, 0, x_ref.shape[0], x, y, g_ref, b_ref)


def _const_spec(shape):
    zeros = (0,) * len(shape)
    return pl.BlockSpec(shape, lambda *_: zeros, pipeline_mode=pl.Buffered(1))


def _ffn_ln(x2d, w_up, w_down, g, b, rows):
    n = x2d.shape[0]
    assert n % rows == 0 and rows % 2 == 0
    n_tiles = n // rows
    half = rows // 2
    weight_specs = [
        _const_spec((D_MODEL, 2 * D_FF)),
        _const_spec((D_FF, D_MODEL)),
        _const_spec((1, D_MODEL)),
        _const_spec((1, D_MODEL)),
    ]
    params = pltpu.CompilerParams(
        dimension_semantics=("arbitrary",), vmem_limit_bytes=VMEM_LIMIT_BYTES)
    if n_tiles == 1:
        return pl.pallas_call(
            _ffn_ln_single_kernel,
            grid=(1,),
            in_specs=[pl.BlockSpec((rows, D_MODEL), lambda i: (0, 0))] + weight_specs,
            out_specs=pl.BlockSpec((rows, D_MODEL), lambda i: (0, 0)),
            out_shape=jax.ShapeDtypeStruct((n, D_MODEL), F32),
            compiler_params=params,
            name="ffn_ln_single",
        )(x2d, w_up, w_down, g, b)
    return pl.pallas_call(
        _ffn_ln_kernel,
        grid=(n_tiles,),
        in_specs=[
            pl.BlockSpec((rows, D_MODEL), lambda i: (i, 0)),
            pl.BlockSpec((half, D_MODEL), lambda i: (2 * jnp.minimum(i + 1, n_tiles - 1), 0)),
        ] + weight_specs,
        out_specs=pl.BlockSpec((rows, D_MODEL), lambda i: (i, 0)),
        out_shape=jax.ShapeDtypeStruct((n, D_MODEL), F32),
        scratch_shapes=[pltpu.VMEM((half, D_FF), BF16)],
        compiler_params=params,
        name="ffn_ln",
    )(x2d, x2d, w_up, w_down, g, b)


def _mixer_kernel(*refs, T, pos0, zero_init):
    if zero_init:
        (x_ref, xnext_ref, wa_ref, wb_ref, convw_ref, g_ref, b_ref,
         y_ref, cbuf_ref, s_ref, rope_scr, dec_scr, rowdec_scr, qk_scr) = refs
    else:
        (x_ref, xnext_ref, wa_ref, wb_ref, convw_ref, g_ref, b_ref,
         cbuf0_ref, s0_ref,
         y_ref, cbuf_ref, s_ref, rope_scr, dec_scr, rowdec_scr, qk_scr) = refs
    t = pl.program_id(1)

    @pl.when(t == 0)
    def _():
        if zero_init:
            cbuf_ref[...] = jnp.zeros_like(cbuf_ref)
            s_ref[...] = jnp.zeros_like(s_ref)
        else:
            cbuf_ref[...] = cbuf0_ref[...]
            s_ref[...] = s0_ref[...]

    half = DK // 2
    b = pl.program_id(0)

    @pl.when(b == 0)
    def _():
        frame = lax.broadcasted_iota(jnp.int32, (T, half), 0)
        pos = (frame + (pos0 + t * T)).astype(F32)
        lane = lax.broadcasted_iota(jnp.int32, (T, half), 1).astype(F32)
        ang = pos * jnp.exp(lane * (-math.log(ROPE_BASE) / half))
        rows = pl.ds(pl.multiple_of(t * T, T), T)
        rope_scr[0, rows, :] = jnp.cos(ang)
        rope_scr[1, rows, :] = jnp.sin(ang)
        rope_scr[2, rows, :] = jnp.cos(ang) * K_SCALE
        rope_scr[3, rows, :] = jnp.sin(ang) * K_SCALE

    @pl.when((b == 0) & (t == 0))
    def _():
        li = lax.broadcasted_iota(jnp.int32, (T, T), 0)
        mi = lax.broadcasted_iota(jnp.int32, (T, T), 1)
        diff = (li - mi).astype(F32)
        idx = lax.broadcasted_iota(jnp.int32, (T, LANES), 0).astype(F32)
        for h in range(N_HEADS):
            lg = LOG_GAMMA[h]
            dec_scr[h] = jnp.where(diff >= 0, jnp.exp(lg * jnp.maximum(diff, 0.0)), 0.0)
            rowdec_scr[0, h] = jnp.exp(lg * (idx + 1.0))
            rowdec_scr[1, h] = jnp.exp(lg * (T - 1.0 - idx))

    x = x_ref[0]
    xb = x.astype(BF16)

    rows = pl.ds(pl.multiple_of(t * T, T), T)

    def weight(lo, width):
        if lo + width <= W_SPLIT:
            return wa_ref[:, lo:lo + width]
        if lo >= OFF_WRB:
            return wb_ref[:, lo - OFF_WRB + D_IN - W_SPLIT:lo - OFF_WRB + D_IN - W_SPLIT + width]
        if lo >= OFF_WCB:
            return wa_ref[:, lo - OFF_WCB + W_SPLIT:lo - OFF_WCB + W_SPLIT + width]
        assert lo >= W_SPLIT
        return wb_ref[:, lo - W_SPLIT:lo - W_SPLIT + width]

    def proj(lo, width, lhs=xb):
        return _dot(lhs, weight(lo, width))

    @pl.when((b == 0) & (t == 0))
    def _():
        qk_scr[...] = proj(OFF_Q, 2 * D_QK)

    def rotary(a, plane):
        cos, sin = rope_scr[plane, rows, :], rope_scr[plane + 1, rows, :]
        a1, a2 = a[:, :half], a[:, half:]
        return jnp.concatenate([a1 * cos - a2 * sin, a1 * sin + a2 * cos], axis=-1)

    def lane_tiled(a, width):
        return jnp.concatenate([a] * (width // LANES), axis=-1)

    def conv_input():
        u = proj(OFF_CC, D_CONV) * proj(OFF_CH, D_CONV)
        prev = cbuf_ref[0]
        row = lax.broadcasted_iota(jnp.int32, (T, D_CONV), 0)
        u1 = jnp.where(row == 0, prev[1:2], pltpu.roll(u, 1, 0))
        u2 = jnp.where(row == 0, prev[0:1], jnp.where(row == 1, prev[1:2], pltpu.roll(u, 2, 0)))
        cw = convw_ref[...]
        cbuf_ref[0] = u[T - 2:T]
        return u2 * cw[0:1] + u1 * cw[1:2] + u * cw[2:3]

    def retention_scores(h, q, k):
        qh = rotary(q[:, h * DK:(h + 1) * DK], 0)
        kh = rotary(k[:, h * DK:(h + 1) * DK], 2)
        qb = qh.astype(BF16)
        scores = lax.dot_general(qb, kh.astype(BF16), (((1,), (1,)), ((), ())),
                                 preferred_element_type=F32) * dec_scr[h]
        return qh, kh, qb, scores.astype(BF16)

    def retention_output(h, qh, kh, qb, scores, v):
        lg = LOG_GAMMA[h]
        vb = v[h].astype(BF16)
        s_old = s_ref[0, h]
        if T % LANES == 0:
            qc = (qh * lane_tiled(rowdec_scr[0, h], DK)).astype(BF16)
            o = _dot(jnp.concatenate([scores, qc], axis=-1),
                     jnp.concatenate([vb, s_old.astype(BF16)], axis=0))
        else:
            intra = _dot(scores, vb)
            cross = _dot(qb, s_old.astype(BF16)) * lane_tiled(rowdec_scr[0, h], DV)
            o = intra + cross
        kdec = (kh * lane_tiled(rowdec_scr[1, h], DK)).astype(BF16)
        s_ref[0, h] = math.exp(lg * T) * s_old + lax.dot_general(
            kdec, vb, (((0,), (0,)), ((), ())), preferred_element_type=F32)
        return o

    def normed_gated(o, gh, chained):
        chunk = min(T, NORM_CHUNK) if chained else T
        pieces, carry = [], None
        for r in range(0, T, chunk):
            oc, gc = o[r:r + chunk], gh[r:r + chunk]
            total = jnp.sum(oc, axis=-1, keepdims=True)
            if carry is not None:
                total = total + carry
            d = oc - total * (1.0 / DV)
            var = jnp.mean(d * d, axis=-1, keepdims=True)
            out = d * lax.rsqrt(var + GN_EPS) * (gc * jax.nn.sigmoid(gc))
            pieces.append(out.astype(BF16))
            carry = jnp.minimum(jnp.maximum(out[:, :1], 0.0), 0.0)
        return jnp.concatenate(pieces, axis=0)

    q = qk_scr[:, :D_QK]
    k = qk_scr[:, D_QK:]
    v = [proj(OFF_V + h * DV, DV) for h in range(N_HEADS)]
    gated = []
    yconv = z = m_conv = gate_r = None
    for h in range(N_HEADS):
        head = retention_scores(h, q, k)
        gh = proj(OFF_G + h * DV, DV)
        o = retention_output(h, *head, v)
        if h == 0:
            yconv = conv_input()
        elif h == 1:
            z = (proj(OFF_CB, D_CONV) * yconv).astype(BF16)
        elif h == 2:
            m_conv = jax.nn.sigmoid(proj(OFF_GC, D_MODEL)) * _dot(z, weight(OFF_WCB, D_MODEL))
        else:
            gate_r = jax.nn.sigmoid(proj(OFF_GR, D_MODEL))
        gated.append(normed_gated(o, gh, chained=h < N_HEADS - 1))
    y_ret = (_dot(jnp.concatenate(gated[:2], axis=-1), weight(OFF_WRB, D_MODEL))
             + _dot(jnp.concatenate(gated[2:], axis=-1), weight(OFF_WRB + D_MODEL, D_MODEL)))

    m = m_conv + gate_r * y_ret
    hmix = _dot(m.astype(BF16), weight(OFF_WOUT, D_MODEL))
    qk_scr[...] = proj(OFF_Q, 2 * D_QK, xnext_ref[0].astype(BF16))
    _residual_norm(y_ref.at[0], 0, T, x, hmix, g_ref, b_ref)


def _mixer(x, slab_a, slab_b, conv_w, g, b, T, pos0, init=None):
    B, seq, _ = x.shape
    assert seq % T == 0
    zero_init = init is None
    n_t = seq // T

    def next_step(i, t):
        lin = jnp.minimum(i * n_t + t + 1, B * n_t - 1)
        return (lin // n_t, lin % n_t, 0)

    in_specs = [
        pl.BlockSpec((1, T, D_MODEL), lambda i, t: (i, t, 0)),
        pl.BlockSpec((1, T, D_MODEL), next_step),
        _const_spec((D_MODEL, SLAB_COLS)),
        _const_spec((D_MODEL, SLAB_COLS)),
        _const_spec((CONV_W, D_CONV)),
        _const_spec((1, D_MODEL)),
        _const_spec((1, D_MODEL)),
    ]
    args = [x, x, slab_a, slab_b, conv_w, g, b]
    cbuf_spec = pl.BlockSpec((1, CONV_W - 1, D_CONV), lambda i, t: (i, 0, 0))
    s_spec = pl.BlockSpec((1, N_HEADS, DK, DV), lambda i, t: (i, 0, 0, 0))
    if not zero_init:
        in_specs += [cbuf_spec, s_spec]
        args += list(init)
    return pl.pallas_call(
        functools.partial(_mixer_kernel, T=T, pos0=pos0, zero_init=zero_init),
        grid=(B, seq // T),
        in_specs=in_specs,
        out_specs=[pl.BlockSpec((1, T, D_MODEL), lambda i, t: (i, t, 0)), cbuf_spec, s_spec],
        out_shape=[
            jax.ShapeDtypeStruct((B, seq, D_MODEL), F32),
            jax.ShapeDtypeStruct((B, CONV_W - 1, D_CONV), F32),
            jax.ShapeDtypeStruct((B, N_HEADS, DK, DV), F32),
        ],
        scratch_shapes=[
            pltpu.VMEM((4, seq, DK // 2), F32),
            pltpu.VMEM((N_HEADS, T, T), F32),
            pltpu.VMEM((2, N_HEADS, T, LANES), F32),
            pltpu.VMEM((T, 2 * D_QK), F32),
        ],
        compiler_params=pltpu.CompilerParams(
            dimension_semantics=("arbitrary", "arbitrary"), vmem_limit_bytes=VMEM_LIMIT_BYTES),
        name="mixer_prompt" if zero_init else "mixer_sample",
    )(*args)


def kernel(x_prompt, x_sample, cache_conv, state_ret, ln1_g, ln1_b, ffn1_w_up, ffn1_w_down, w_in,
           conv_w, w_conv_br, w_ret_br, w_out, ln2_g, ln2_b, ffn2_w_up, ffn2_w_down, ln3_g, ln3_b):
    assert w_in.shape[0] == DEPTH == 1
    Bp, Tp, _ = x_prompt.shape
    Bs, Ts, _ = x_sample.shape

    def cast(w, scale=1.0):
        rows, cols = w.shape[1:]
        return _pack_bf16([(w[0], 0, 0, cols, scale)], rows)

    wu1, wd1 = cast(ffn1_w_up), cast(ffn1_w_down, 0.5 / ALPHA)
    wu2, wd2 = cast(ffn2_w_up), cast(ffn2_w_down, 0.5 / ALPHA)
    slab_a = _pack_bf16([(w_in[0], 0, 0, W_SPLIT, 1.0), (w_conv_br[0], 0, 0, D_MODEL, 1.0),
                         (w_out[0], 0, 0, D_MODEL, 1.0 / ALPHA)], D_MODEL)
    slab_b = _pack_bf16([(w_in[0], 0, W_SPLIT, D_IN - W_SPLIT, 1.0),
                         (w_ret_br[0], 0, 0, D_MODEL, 1.0),
                         (w_ret_br[0], D_MODEL, 0, D_MODEL, 1.0)], D_MODEL)

    def layer(x, T, pos0, rows, init):
        B, seq, _ = x.shape
        h = _ffn_ln(x.reshape(B * seq, D_MODEL), wu1, wd1, ln1_g, ln1_b, rows)
        h, cbuf, s = _mixer(h.reshape(B, seq, D_MODEL), slab_a, slab_b, conv_w[0], ln2_g, ln2_b,
                            T, pos0, init)
        h = _ffn_ln(h.reshape(B * seq, D_MODEL), wu2, wd2, ln3_g, ln3_b, rows)
        return h.reshape(B, seq, D_MODEL), cbuf[None], s[None]

    yp, conv_p, ret_p = layer(x_prompt, MIXER_ROWS, 0, FFN_ROWS, None)
    ys, conv_s, ret_s = layer(x_sample, Ts, PAST_LEN, Bs * Ts, (cache_conv[0], state_ret[0]))
    return (yp, ys, conv_p, ret_p, conv_s, ret_s)
```

```python
import functools
import math

import jax
import jax.numpy as jnp
from jax import lax
from jax.experimental import pallas as pl
from jax.experimental.pallas import tpu as pltpu

D_MODEL = 1024
D_FF = 2816
D_CONV = 1024
CONV_W = 3
N_HEADS = 4
DK = 256
DV = 512
D_QK = N_HEADS * DK
D_V = N_HEADS * DV
PAST_LEN = 2048
ROPE_BASE = 10000.0
LN_EPS = 1e-5
GN_EPS = 1e-6
DEPTH = 1
ALPHA = (2 * DEPTH) ** 0.25
K_SCALE = DK ** -0.5
assert math.frexp(K_SCALE)[0] == 0.5, "the key scale is folded into the rotary tables"

OFF_CB = 0
OFF_CC = OFF_CB + D_CONV
OFF_CH = OFF_CC + D_CONV
OFF_Q = OFF_CH + D_CONV
OFF_K = OFF_Q + D_QK
OFF_V = OFF_K + D_QK
OFF_G = OFF_V + D_V
OFF_GC = OFF_G + D_V
OFF_GR = OFF_GC + D_MODEL
D_IN = OFF_GR + D_MODEL

LOG_GAMMA = tuple(math.log(1.0 - 2.0 ** (-5.0 - h)) for h in range(N_HEADS))

LANES = 128
W_SPLIT = D_IN // 2
SLAB_COLS = W_SPLIT + 2 * D_MODEL
assert (SLAB_COLS // LANES) % 8 != 0 and D_V == 2 * D_MODEL
OFF_WCB = D_IN
OFF_WOUT = OFF_WCB + D_MODEL
OFF_WRB = OFF_WOUT + D_MODEL
NORM_CHUNK = 64
PACK_STEPS = 4
FFN_ROWS = 1024
MIXER_ROWS = 256
VMEM_LIMIT_BYTES = 58 * 1024 * 1024

F32 = jnp.float32
BF16 = jnp.bfloat16


def _residual_norm(o_ref, start, n, x, h, g_ref, b_ref):
    o_ref[pl.ds(start, n), :] = x + h
    g, b = g_ref[...], b_ref[...]
    chunk = min(n, NORM_CHUNK)
    carry = None
    for r in range(start, start + n, chunk):
        z = o_ref[pl.ds(r, chunk), :]
        total = jnp.sum(z, axis=-1, keepdims=True)
        if carry is not None:
            total = total + carry
        mu = total * (1.0 / D_MODEL)
        d = z - mu
        var = jnp.mean(d * d, axis=-1, keepdims=True)
        out = d * lax.rsqrt(var + LN_EPS / ALPHA ** 2) * g + b
        o_ref[pl.ds(r, chunk), :] = out
        carry = jnp.minimum(jnp.maximum(out[:, :1], 0.0), 0.0)


def _dot(a, b):
    return jnp.dot(a, b, preferred_element_type=F32)


def _swish_times(a, b):
    t = 0.5 * a
    return (jnp.tanh(t) + 1.0) * t * b


def _ffn_ln_kernel(x_ref, xnext_ref, wu_ref, wd_ref, g_ref, b_ref, o_ref, act_scr):
    half = x_ref.shape[0] // 2

    def gated_up(x):
        xb = x.astype(BF16)
        a = _dot(xb, wu_ref[:, :D_FF])
        b = _dot(xb, wu_ref[:, D_FF:])
        return _swish_times(a, b).astype(BF16)

    def down(act):
        return _dot(act, wd_ref[:, :D_MODEL])

    @pl.when(pl.program_id(0) == 0)
    def _():
        act_scr[...] = gated_up(x_ref[:half, :])

    x0 = x_ref[:half, :]
    x1 = x_ref[half:, :]
    y0 = down(act_scr[...])
    y1 = down(gated_up(x1))
    act_scr[...] = gated_up(xnext_ref[...])
    _residual_norm(o_ref, 0, half, x0, y0, g_ref, b_ref)
    _residual_norm(o_ref, half, half, x1, y1, g_ref, b_ref)


def _pack_kernel(*refs, scales):
    *piece_refs, o_ref = refs
    col = 0
    for p_ref, scale in zip(piece_refs, scales):
        width = p_ref.shape[-1]
        piece = p_ref[...] if scale == 1.0 else p_ref[...] * scale
        o_ref[:, col:col + width] = piece.astype(BF16)
        col += width


def _pack_bf16(pieces, rows):
    assert rows % (PACK_STEPS * 16) == 0
    step_rows = rows // PACK_STEPS
    in_specs, args = [], []
    for w, row0, col0, width, _ in pieces:
        assert row0 % step_rows == 0 and col0 % width == 0
        in_specs.append(pl.BlockSpec(
            (step_rows, width),
            functools.partial(lambda i, r, c: (i + r, c), r=row0 // step_rows, c=col0 // width)))
        args.append(w)
    total = sum(p[3] for p in pieces)
    return pl.pallas_call(
        functools.partial(_pack_kernel, scales=tuple(p[4] for p in pieces)),
        grid=(PACK_STEPS,),
        in_specs=in_specs,
        out_specs=pl.BlockSpec((step_rows, total), lambda i: (i, 0)),
        out_shape=jax.ShapeDtypeStruct((rows, total), BF16),
        compiler_params=pltpu.CompilerParams(
            dimension_semantics=("arbitrary",), vmem_limit_bytes=VMEM_LIMIT_BYTES),
        name="pack_bf16",
    )(*args)


def _ffn_ln_single_kernel(x_ref, wu_ref, wd_ref, g_ref, b_ref, o_ref):
    x = x_ref[...]
    xb = x.astype(BF16)
    a = _dot(xb, wu_ref[:, :D_FF])
    b = _dot(xb, wu_ref[:, D_FF:])
    y = _dot(_swish_times(a, b).astype(BF16), wd_ref[:, :D_MODEL])
    _residual_norm(o_ref, 0, x_ref.shape[0], x, y, g_ref, b_ref)


def _const_spec(shape):
    zeros = (0,) * len(shape)
    return pl.BlockSpec(shape, lambda *_: zeros, pipeline_mode=pl.Buffered(1))


def _ffn_ln(x2d, w_up, w_down, g, b, rows):
    n = x2d.shape[0]
    assert n % rows == 0 and rows % 2 == 0
    n_tiles = n // rows
    half = rows // 2
    weight_specs = [
        _const_spec((D_MODEL, 2 * D_FF)),
        _const_spec((D_FF, D_MODEL + LANES)),
        _const_spec((1, D_MODEL)),
        _const_spec((1, D_MODEL)),
    ]
    params = pltpu.CompilerParams(
        dimension_semantics=("arbitrary",), vmem_limit_bytes=VMEM_LIMIT_BYTES)
    if n_tiles == 1:
        return pl.pallas_call(
            _ffn_ln_single_kernel,
            grid=(1,),
            in_specs=[pl.BlockSpec((rows, D_MODEL), lambda i: (0, 0))] + weight_specs,
            out_specs=pl.BlockSpec((rows, D_MODEL), lambda i: (0, 0)),
            out_shape=jax.ShapeDtypeStruct((n, D_MODEL), F32),
            compiler_params=params,
            name="ffn_ln_single",
        )(x2d, w_up, w_down, g, b)
    return pl.pallas_call(
        _ffn_ln_kernel,
        grid=(n_tiles,),
        in_specs=[
            pl.BlockSpec((rows, D_MODEL), lambda i: (i, 0)),
            pl.BlockSpec((half, D_MODEL), lambda i: (2 * jnp.minimum(i + 1, n_tiles - 1), 0)),
        ] + weight_specs,
        out_specs=pl.BlockSpec((rows, D_MODEL), lambda i: (i, 0)),
        out_shape=jax.ShapeDtypeStruct((n, D_MODEL), F32),
        scratch_shapes=[pltpu.VMEM((half, D_FF), BF16)],
        compiler_params=params,
        name="ffn_ln",
    )(x2d, x2d, w_up, w_down, g, b)


def _mixer_kernel(*refs, T, pos0, zero_init):
    if zero_init:
        (x_ref, xnext_ref, wa_ref, wb_ref, convw_ref, g_ref, b_ref,
         y_ref, cbuf_ref, s_ref, rope_scr, dec_scr, rowdec_scr, qk_scr) = refs
    else:
        (x_ref, xnext_ref, wa_ref, wb_ref, convw_ref, g_ref, b_ref,
         cbuf0_ref, s0_ref,
         y_ref, cbuf_ref, s_ref, rope_scr, dec_scr, rowdec_scr, qk_scr) = refs
    t = pl.program_id(1)

    @pl.when(t == 0)
    def _():
        if zero_init:
            cbuf_ref[...] = jnp.zeros_like(cbuf_ref)
            s_ref[...] = jnp.zeros_like(s_ref)
        else:
            cbuf_ref[...] = cbuf0_ref[...]
            s_ref[...] = s0_ref[...]

    half = DK // 2
    b = pl.program_id(0)

    @pl.when(b == 0)
    def _():
        frame = lax.broadcasted_iota(jnp.int32, (T, half), 0)
        pos = (frame + (pos0 + t * T)).astype(F32)
        lane = lax.broadcasted_iota(jnp.int32, (T, half), 1).astype(F32)
        ang = pos * jnp.exp(lane * (-math.log(ROPE_BASE) / half))
        rows = pl.ds(pl.multiple_of(t * T, T), T)
        rope_scr[0, rows, :] = jnp.cos(ang)
        rope_scr[1, rows, :] = jnp.sin(ang)
        rope_scr[2, rows, :] = jnp.cos(ang) * K_SCALE
        rope_scr[3, rows, :] = jnp.sin(ang) * K_SCALE

    @pl.when((b == 0) & (t == 0))
    def _():
        li = lax.broadcasted_iota(jnp.int32, (T, T), 0)
        mi = lax.broadcasted_iota(jnp.int32, (T, T), 1)
        diff = (li - mi).astype(F32)
        idx = lax.broadcasted_iota(jnp.int32, (T, LANES), 0).astype(F32)
        for h in range(N_HEADS):
            lg = LOG_GAMMA[h]
            dec_scr[h] = jnp.where(diff >= 0, jnp.exp(lg * jnp.maximum(diff, 0.0)), 0.0)
            rowdec_scr[0, h] = jnp.exp(lg * (idx + 1.0))
            rowdec_scr[1, h] = jnp.exp(lg * (T - 1.0 - idx))

    x = x_ref[0]
    xb = x.astype(BF16)

    rows = pl.ds(pl.multiple_of(t * T, T), T)

    def weight(lo, width):
        if lo + width <= W_SPLIT:
            return wa_ref[:, lo:lo + width]
        if lo >= OFF_WRB:
            return wb_ref[:, lo - OFF_WRB + D_IN - W_SPLIT:lo - OFF_WRB + D_IN - W_SPLIT + width]
        if lo >= OFF_WCB:
            return wa_ref[:, lo - OFF_WCB + W_SPLIT:lo - OFF_WCB + W_SPLIT + width]
        assert lo >= W_SPLIT
        return wb_ref[:, lo - W_SPLIT:lo - W_SPLIT + width]

    def proj(lo, width, lhs=xb):
        return _dot(lhs, weight(lo, width))

    @pl.when((b == 0) & (t == 0))
    def _():
        qk_scr[...] = proj(OFF_Q, 2 * D_QK)

    def rotary(a, plane):
        cos, sin = rope_scr[plane, rows, :], rope_scr[plane + 1, rows, :]
        a1, a2 = a[:, :half], a[:, half:]
        return jnp.concatenate([a1 * cos - a2 * sin, a1 * sin + a2 * cos], axis=-1)

    def lane_tiled(a, width):
        return jnp.concatenate([a] * (width // LANES), axis=-1)

    def conv_input():
        u = proj(OFF_CC, D_CONV) * proj(OFF_CH, D_CONV)
        prev = cbuf_ref[0]
        row = lax.broadcasted_iota(jnp.int32, (T, D_CONV), 0)
        u1 = jnp.where(row == 0, prev[1:2], pltpu.roll(u, 1, 0))
        u2 = jnp.where(row == 0, prev[0:1], jnp.where(row == 1, prev[1:2], pltpu.roll(u, 2, 0)))
        cw = convw_ref[...]
        cbuf_ref[0] = u[T - 2:T]
        return u2 * cw[0:1] + u1 * cw[1:2] + u * cw[2:3]

    def retention_scores(h, q, k):
        qh = rotary(q[:, h * DK:(h + 1) * DK], 0)
        kh = rotary(k[:, h * DK:(h + 1) * DK], 2)
        qb = qh.astype(BF16)
        scores = lax.dot_general(qb, kh.astype(BF16), (((1,), (1,)), ((), ())),
                                 preferred_element_type=F32) * dec_scr[h]
        return qh, kh, qb, scores.astype(BF16)

    def retention_output(h, qh, kh, qb, scores, v):
        lg = LOG_GAMMA[h]
        vb = v[h].astype(BF16)
        s_old = s_ref[0, h]
        if T % LANES == 0:
            qc = (qh * lane_tiled(rowdec_scr[0, h], DK)).astype(BF16)
            o = _dot(jnp.concatenate([scores, qc], axis=-1),
                     jnp.concatenate([vb, s_old.astype(BF16)], axis=0))
        else:
            intra = _dot(scores, vb)
            cross = _dot(qb, s_old.astype(BF16)) * lane_tiled(rowdec_scr[0, h], DV)
            o = intra + cross
        kdec = (kh * lane_tiled(rowdec_scr[1, h], DK)).astype(BF16)
        s_ref[0, h] = math.exp(lg * T) * s_old + lax.dot_general(
            kdec, vb, (((0,), (0,)), ((), ())), preferred_element_type=F32)
        return o

    def normed_gated(o, gh, chained):
        chunk = min(T, NORM_CHUNK) if chained else T
        pieces, carry = [], None
        for r in range(0, T, chunk):
            oc, gc = o[r:r + chunk], gh[r:r + chunk]
            total = jnp.sum(oc, axis=-1, keepdims=True)
            if carry is not None:
                total = total + carry
            d = oc - total * (1.0 / DV)
            var = jnp.mean(d * d, axis=-1, keepdims=True)
            out = d * lax.rsqrt(var + GN_EPS) * (gc * jax.nn.sigmoid(gc))
            pieces.append(out.astype(BF16))
            carry = jnp.minimum(jnp.maximum(out[:, :1], 0.0), 0.0)
        return jnp.concatenate(pieces, axis=0)

    q = qk_scr[:, :D_QK]
    k = qk_scr[:, D_QK:]
    v = [proj(OFF_V + h * DV, DV) for h in range(N_HEADS)]
    gated = []
    yconv = z = m_conv = gate_r = None
    for h in range(N_HEADS):
        head = retention_scores(h, q, k)
        gh = proj(OFF_G + h * DV, DV)
        o = retention_output(h, *head, v)
        if h == 0:
            yconv = conv_input()
        elif h == 1:
            z = (proj(OFF_CB, D_CONV) * yconv).astype(BF16)
        elif h == 2:
            m_conv = jax.nn.sigmoid(proj(OFF_GC, D_MODEL)) * _dot(z, weight(OFF_WCB, D_MODEL))
        else:
            gate_r = jax.nn.sigmoid(proj(OFF_GR, D_MODEL))
        gated.append(normed_gated(o, gh, chained=h < N_HEADS - 1))
    y_ret = (_dot(jnp.concatenate(gated[:2], axis=-1), weight(OFF_WRB, D_MODEL))
             + _dot(jnp.concatenate(gated[2:], axis=-1), weight(OFF_WRB + D_MODEL, D_MODEL)))

    m = m_conv + gate_r * y_ret
    hmix = _dot(m.astype(BF16), weight(OFF_WOUT, D_MODEL))
    qk_scr[...] = proj(OFF_Q, 2 * D_QK, xnext_ref[0].astype(BF16))
    _residual_norm(y_ref.at[0], 0, T, x, hmix, g_ref, b_ref)


def _mixer(x, slab_a, slab_b, conv_w, g, b, T, pos0, init=None):
    B, seq, _ = x.shape
    assert seq % T == 0
    zero_init = init is None
    n_t = seq // T

    def next_step(i, t):
        lin = jnp.minimum(i * n_t + t + 1, B * n_t - 1)
        return (lin // n_t, lin % n_t, 0)

    in_specs = [
        pl.BlockSpec((1, T, D_MODEL), lambda i, t: (i, t, 0)),
        pl.BlockSpec((1, T, D_MODEL), next_step),
        _const_spec((D_MODEL, SLAB_COLS)),
        _const_spec((D_MODEL, SLAB_COLS)),
        _const_spec((CONV_W, D_CONV)),
        _const_spec((1, D_MODEL)),
        _const_spec((1, D_MODEL)),
    ]
    args = [x, x, slab_a, slab_b, conv_w, g, b]
    cbuf_spec = pl.BlockSpec((1, CONV_W - 1, D_CONV), lambda i, t: (i, 0, 0))
    s_spec = pl.BlockSpec((1, N_HEADS, DK, DV), lambda i, t: (i, 0, 0, 0))
    if not zero_init:
        in_specs += [cbuf_spec, s_spec]
        args += list(init)
    return pl.pallas_call(
        functools.partial(_mixer_kernel, T=T, pos0=pos0, zero_init=zero_init),
        grid=(B, seq // T),
        in_specs=in_specs,
        out_specs=[pl.BlockSpec((1, T, D_MODEL), lambda i, t: (i, t, 0)), cbuf_spec, s_spec],
        out_shape=[
            jax.ShapeDtypeStruct((B, seq, D_MODEL), F32),
            jax.ShapeDtypeStruct((B, CONV_W - 1, D_CONV), F32),
            jax.ShapeDtypeStruct((B, N_HEADS, DK, DV), F32),
        ],
        scratch_shapes=[
            pltpu.VMEM((4, seq, DK // 2), F32),
            pltpu.VMEM((N_HEADS, T, T), F32),
            pltpu.VMEM((2, N_HEADS, T, LANES), F32),
            pltpu.VMEM((T, 2 * D_QK), F32),
        ],
        compiler_params=pltpu.CompilerParams(
            dimension_semantics=("arbitrary", "arbitrary"), vmem_limit_bytes=VMEM_LIMIT_BYTES),
        name="mixer_prompt" if zero_init else "mixer_sample",
    )(*args)


def kernel(x_prompt, x_sample, cache_conv, state_ret, ln1_g, ln1_b, ffn1_w_up, ffn1_w_down, w_in,
           conv_w, w_conv_br, w_ret_br, w_out, ln2_g, ln2_b, ffn2_w_up, ffn2_w_down, ln3_g, ln3_b):
    assert w_in.shape[0] == DEPTH == 1
    Bp, Tp, _ = x_prompt.shape
    Bs, Ts, _ = x_sample.shape

    def cast(w, scale=1.0):
        rows, cols = w.shape[1:]
        return _pack_bf16([(w[0], 0, 0, cols, scale)], rows)

    def cast_wide(w, scale):
        return _pack_bf16([(w[0], 0, 0, D_MODEL, scale), (w[0], 0, 0, LANES, scale)], D_FF)

    wu1, wd1 = cast(ffn1_w_up), cast_wide(ffn1_w_down, 0.5 / ALPHA)
    wu2, wd2 = cast(ffn2_w_up), cast_wide(ffn2_w_down, 0.5 / ALPHA)
    slab_a = _pack_bf16([(w_in[0], 0, 0, W_SPLIT, 1.0), (w_conv_br[0], 0, 0, D_MODEL, 1.0),
                         (w_out[0], 0, 0, D_MODEL, 1.0 / ALPHA)], D_MODEL)
    slab_b = _pack_bf16([(w_in[0], 0, W_SPLIT, D_IN - W_SPLIT, 1.0),
                         (w_ret_br[0], 0, 0, D_MODEL, 1.0),
                         (w_ret_br[0], D_MODEL, 0, D_MODEL, 1.0)], D_MODEL)

    def layer(x, T, pos0, rows, init):
        B, seq, _ = x.shape
        h = _ffn_ln(x.reshape(B * seq, D_MODEL), wu1, wd1, ln1_g, ln1_b, rows)
        h, cbuf, s = _mixer(h.reshape(B, seq, D_MODEL), slab_a, slab_b, conv_w[0], ln2_g, ln2_b,
                            T, pos0, init)
        h = _ffn_ln(h.reshape(B * seq, D_MODEL), wu2, wd2, ln3_g, ln3_b, rows)
        return h.reshape(B, seq, D_MODEL), cbuf[None], s[None]

    yp, conv_p, ret_p = layer(x_prompt, MIXER_ROWS, 0, FFN_ROWS, None)
    ys, conv_s, ret_s = layer(x_sample, Ts, PAST_LEN, Bs * Ts, (cache_conv[0], state_ret[0]))
    return (yp, ys, conv_p, ret_p, conv_s, ret_s)
```
